```python
import math
import jax, jax.numpy as jnp
from jax import lax
import numpy as np

D_MODEL = 1024
BATCH = 8
SEQ = 4096
DEPTH = 4

CHUNK = 64
MEM_LEN = 256
N_MIXERS = 2
SSM_GROUP = 16
SSM_GROUPS = D_MODEL // SSM_GROUP
SSM_STATE = 64
SB_HEADS = 16
SB_HEAD_DIM = D_MODEL // SB_HEADS
Q_BLOCK = 128
XA_HEADS = 4
XA_HEAD_DIM = D_MODEL // XA_HEADS
FFN_DIM = ((8 * D_MODEL // 3 + 127) // 128) * 128
CONV_WIDTH = 3
LN_EPS = 1e-5
DN_ALPHA = (2.0 * DEPTH) ** 0.25
DN_BETA = (8.0 * DEPTH) ** -0.25
N_S5 = (DEPTH + 1) // 2
N_SB = DEPTH // 2

kernel_name = "hybrid_s5_stickbreaking_deepnorm_encoder"


def layer_norm(x, g, b):
    xf = x.astype(jnp.float32)
    mu = jnp.mean(xf, axis=-1, keepdims=True)
    var = jnp.mean(jnp.square(xf - mu), axis=-1, keepdims=True)
    return ((xf - mu) * lax.rsqrt(var + LN_EPS) * g + b).astype(x.dtype)


def deepnorm_residual(x, h, g, b):
    return layer_norm(DN_ALPHA * x + h, g, b)


def _complex_linear_combine(e1, e2):
    a1r, a1i, b1r, b1i = e1
    a2r, a2i, b2r, b2i = e2
    ar = a2r * a1r - a2i * a1i
    ai = a2r * a1i + a2i * a1r
    br = a2r * b1r - a2i * b1i + b2r
    bi = a2r * b1i + a2i * b1r + b2i
    return (ar, ai, br, bi)


def s5_mixer(x, w_in, a_re, a_im, log_step, b_re, b_im, c_re, c_im, d_skip, w_out):
    f32 = jnp.float32
    bsz, seq, _ = x.shape
    n_chunks = seq // CHUNK
    u = (x @ w_in).astype(f32)
    a_re = a_re.astype(f32); a_im = a_im.astype(f32)
    step = jnp.exp(log_step.astype(f32))[:, None]
    mag = jnp.exp(a_re * step)
    ang = a_im * step
    lam_r = mag * jnp.cos(ang)
    lam_i = mag * jnp.sin(ang)
    den = a_re * a_re + a_im * a_im
    nr = lam_r - 1.0
    ni = lam_i
    f_r = (nr * a_re + ni * a_im) / den
    f_i = (ni * a_re - nr * a_im) / den
    b_re = b_re.astype(f32); b_im = b_im.astype(f32)
    bb_r = f_r[..., None] * b_re - f_i[..., None] * b_im
    bb_i = f_r[..., None] * b_im + f_i[..., None] * b_re
    c_r = c_re.astype(f32); c_i = c_im.astype(f32)
    lam_seq_r = jnp.broadcast_to(lam_r, (CHUNK, 1, SSM_GROUPS, SSM_STATE))
    lam_seq_i = jnp.broadcast_to(lam_i, (CHUNK, 1, SSM_GROUPS, SSM_STATE))
    u_c = u.reshape(bsz, n_chunks, CHUNK, SSM_GROUPS, SSM_GROUP).transpose(1, 2, 0, 3, 4)

    def chunk_step(carry, u_t):
        h_r, h_i = carry
        bu_r = jnp.einsum('tbgc,gpc->tbgp', u_t, bb_r)
        bu_i = jnp.einsum('tbgc,gpc->tbgp', u_t, bb_i)
        bu_r = bu_r.at[0].add(lam_r * h_r - lam_i * h_i)
        bu_i = bu_i.at[0].add(lam_r * h_i + lam_i * h_r)
        _, _, s_r, s_i = lax.associative_scan(
            _complex_linear_combine, (lam_seq_r, lam_seq_i, bu_r, bu_i), axis=0)
        y = (jnp.einsum('tbgp,gcp->tbgc', s_r, c_r)
             - jnp.einsum('tbgp,gcp->tbgc', s_i, c_i))
        return (s_r[-1], s_i[-1]), y

    init = (jnp.zeros((bsz, SSM_GROUPS, SSM_STATE), f32),
            jnp.zeros((bsz, SSM_GROUPS, SSM_STATE), f32))
    _, ys = lax.scan(chunk_step, init, u_c)
    y = ys.transpose(2, 0, 1, 3, 4).reshape(bsz, seq, D_MODEL)
    y = y + d_skip.astype(f32) * u
    y = jax.nn.gelu(y).astype(x.dtype)
    val, gate = jnp.split(y @ w_out, 2, axis=-1)
    return val * jax.nn.sigmoid(gate)


def stick_breaking_attention(x, w_qkv, w_o):
    bsz, seq, _ = x.shape
    q, k, v = jnp.split(x @ w_qkv, 3, axis=-1)
    q = q.reshape(bsz, seq, SB_HEADS, SB_HEAD_DIM)
    k = k.reshape(bsz, seq, SB_HEADS, SB_HEAD_DIM)
    v = v.reshape(bsz, seq, SB_HEADS, SB_HEAD_DIM)
    scale = SB_HEAD_DIM ** -0.5
    outs = []
    for start in range(0, seq, Q_BLOCK):
        end = start + Q_BLOCK
        qb = q[:, start:end]
        kb = k[:, :end]
        vb = v[:, :end]
        z = jnp.einsum('bqhd,bkhd->bhqk', qb, kb).astype(jnp.float32) * scale
        t_idx = start + jnp.arange(Q_BLOCK)[:, None]
        s_idx = jnp.arange(end)[None, :]
        mask = s_idx < t_idx
        log_beta = jax.nn.log_sigmoid(z)
        log_1m = jnp.where(mask, jax.nn.log_sigmoid(-z), 0.0)
        between = lax.cumsum(log_1m, axis=3, reverse=True) - log_1m
        w = jnp.where(mask, jnp.exp(log_beta + between), 0.0)
        outs.append(jnp.einsum('bhqk,bkhd->bqhd', w.astype(vb.dtype), vb))
    o = jnp.concatenate(outs, axis=1).reshape(bsz, seq, D_MODEL)
    return o @ w_o


def memory_cross_attention(x, mem, w_q, w_kv, w_o):
    bsz, seq, _ = x.shape
    q = (x @ w_q).reshape(bsz, seq, XA_HEADS, XA_HEAD_DIM)
    k, v = jnp.split(mem @ w_kv, 2, axis=-1)
    k = k.reshape(bsz, MEM_LEN, XA_HEADS, XA_HEAD_DIM)
    v = v.reshape(bsz, MEM_LEN, XA_HEADS, XA_HEAD_DIM)
    s = jnp.einsum('bqhd,bkhd->bhqk', q, k).astype(jnp.float32) * (XA_HEAD_DIM ** -0.5)
    p = jax.nn.softmax(s, axis=-1).astype(v.dtype)
    o = jnp.einsum('bhqk,bkhd->bqhd', p, v).reshape(bsz, seq, D_MODEL)
    return o @ w_o


def conv_ffn(x, w_up, conv_w, conv_b, w_down):
    u = x @ w_up
    u = lax.conv_general_dilated(
        u, conv_w[:, None, :], window_strides=(1,), padding=[(CONV_WIDTH - 1, 0)],
        dimension_numbers=('NWC', 'WIO', 'NWC'), feature_group_count=2 * FFN_DIM) + conv_b
    a, g = jnp.split(u, 2, axis=-1)
    return (a * jax.nn.gelu(g)) @ w_down


def setup_inputs(seed: int = 0) -> dict:
    key = jax.random.key(seed)
    ks = iter(jax.random.split(key, 32))
    f32 = jnp.float32

    def nrm(shape, scale):
        return jax.random.normal(next(ks), shape, f32) * scale

    D = D_MODEL
    G, P = SSM_GROUPS, SSM_STATE
    x = nrm((BATCH, SEQ, D), 1.0)
    mem = nrm((BATCH, MEM_LEN, D), 1.0)
    s5_w_in = nrm((N_S5, D, D), D ** -0.5)
    s5_a_re = -0.5 + nrm((N_S5, G, P), 0.01)
    s5_a_im = math.pi * jnp.arange(P, dtype=f32) + nrm((N_S5, G, P), 0.01)
    s5_log_step = jax.random.uniform(next(ks), (N_S5, G), f32, math.log(1e-3), math.log(1e-1))
    s5_b_re = nrm((N_S5, G, P, SSM_GROUP), (2.0 * SSM_GROUP) ** -0.5)
    s5_b_im = nrm((N_S5, G, P, SSM_GROUP), (2.0 * SSM_GROUP) ** -0.5)
    s5_c_re = nrm((N_S5, G, SSM_GROUP, P), (2.0 * P) ** -0.5)
    s5_c_im = nrm((N_S5, G, SSM_GROUP, P), (2.0 * P) ** -0.5)
    s5_d = nrm((N_S5, D), 1.0)
    s5_w_out = nrm((N_S5, D, 2 * D), D ** -0.5 * DN_BETA)
    sb_w_qkv = nrm((N_SB, D, 3 * D), D ** -0.5)
    sb_w_o = nrm((N_SB, D, D), D ** -0.5 * DN_BETA)
    xa_w_q = nrm((DEPTH, D, D), D ** -0.5)
    xa_w_kv = nrm((DEPTH, D, 2 * D), D ** -0.5)
    xa_w_o = nrm((DEPTH, D, D), D ** -0.5 * DN_BETA)
    ffn_w_up = nrm((DEPTH, D, 2 * FFN_DIM), D ** -0.5)
    ffn_conv_w = nrm((DEPTH, CONV_WIDTH, 2 * FFN_DIM), CONV_WIDTH ** -0.5)
    ffn_conv_b = nrm((DEPTH, 2 * FFN_DIM), 0.02)
    ffn_w_down = nrm((DEPTH, FFN_DIM, D), FFN_DIM ** -0.5 * DN_BETA)
    ln_g = 1.0 + nrm((DEPTH, 3, D), 0.02)
    ln_b = nrm((DEPTH, 3, D), 0.02)
    return {"x": x, "mem": mem,
            "s5_w_in": s5_w_in, "s5_a_re": s5_a_re, "s5_a_im": s5_a_im,
            "s5_log_step": s5_log_step, "s5_b_re": s5_b_re, "s5_b_im": s5_b_im,
            "s5_c_re": s5_c_re, "s5_c_im": s5_c_im, "s5_d": s5_d, "s5_w_out": s5_w_out,
            "sb_w_qkv": sb_w_qkv, "sb_w_o": sb_w_o,
            "xa_w_q": xa_w_q, "xa_w_kv": xa_w_kv, "xa_w_o": xa_w_o,
            "ffn_w_up": ffn_w_up, "ffn_conv_w": ffn_conv_w, "ffn_conv_b": ffn_conv_b,
            "ffn_w_down": ffn_w_down, "ln_g": ln_g, "ln_b": ln_b}


def reference(x, mem, s5_w_in, s5_a_re, s5_a_im, s5_log_step, s5_b_re, s5_b_im,
              s5_c_re, s5_c_im, s5_d, s5_w_out, sb_w_qkv, sb_w_o,
              xa_w_q, xa_w_kv, xa_w_o, ffn_w_up, ffn_conv_w, ffn_conv_b, ffn_w_down,
              ln_g, ln_b):
    for i in range(DEPTH):
        j = i // N_MIXERS
        if i % N_MIXERS == 0:
            h = s5_mixer(x, s5_w_in[j], s5_a_re[j], s5_a_im[j], s5_log_step[j],
                         s5_b_re[j], s5_b_im[j], s5_c_re[j], s5_c_im[j], s5_d[j], s5_w_out[j])
        else:
            h = stick_breaking_attention(x, sb_w_qkv[j], sb_w_o[j])
        x = deepnorm_residual(x, h, ln_g[i, 0], ln_b[i, 0])
        h = memory_cross_attention(x, mem, xa_w_q[i], xa_w_kv[i], xa_w_o[i])
        x = deepnorm_residual(x, h, ln_g[i, 1], ln_b[i, 1])
        h = conv_ffn(x, ffn_w_up[i], ffn_conv_w[i], ffn_conv_b[i], ffn_w_down[i])
        x = deepnorm_residual(x, h, ln_g[i, 2], ln_b[i, 2])
    return x
```

```python
import functools

import jax
import jax.numpy as jnp
from jax import lax
from jax.experimental import pallas as pl
from jax.experimental.pallas import tpu as pltpu

F32 = jnp.float32
BF16 = jnp.bfloat16

LN_EPS = 1e-5
LANES = 128
SB_HEADS = 16
XA_HEADS = 4
S5_CHUNK = 16
FFN_COLS = 256
SB_BLOCK = 128
SB_SKIP_LOGIT = 105.0
VMEM_LIMIT = 56 * 1024 * 1024


def _params(*sem):
    return pltpu.CompilerParams(dimension_semantics=sem, vmem_limit_bytes=VMEM_LIMIT)


def _layer_norm(y, g, b):
    mu = jnp.mean(y, axis=-1, keepdims=True)
    yc = y - mu
    var = jnp.mean(yc * yc, axis=-1, keepdims=True)
    return yc * lax.rsqrt(var + LN_EPS) * g + b


def _gelu_tanh(x):
    return 0.5 * x * (1.0 + jnp.tanh(0.7978845608028654 * (x + 0.044715 * (x * x * x))))


def _sigmoid(x):
    return 1.0 / (1.0 + jnp.exp(-x))


def _dot(a, b):
    return jnp.dot(a, b, preferred_element_type=F32)


def _dot_nt(a, b):
    return lax.dot_general(a, b, (((1,), (1,)), ((), ())), preferred_element_type=F32)


def _matmul_kernel(x_ref, w_ref, o_ref):
    o_ref[...] = _dot(x_ref[...].astype(BF16), w_ref[...]).astype(o_ref.dtype)


def _matmul(x, w, out_dtype, tm):
    t, k = x.shape
    n = w.shape[1]
    return pl.pallas_call(
        _matmul_kernel,
        grid=(t // tm,),
        in_specs=[pl.BlockSpec((tm, k), lambda i: (i, 0)),
                  pl.BlockSpec((k, n), lambda i: (0, 0))],
        out_specs=pl.BlockSpec((tm, n), lambda i: (i, 0)),
        out_shape=jax.ShapeDtypeStruct((t, n), out_dtype),
        compiler_params=_params("parallel"),
        name="matmul",
    )(x, w)


def _proj_ln_kernel(a_ref, x_ref, w_ref, g_ref, b_ref, o_ref, *, alpha):
    h = _dot(a_ref[...], w_ref[...])
    o_ref[...] = _layer_norm(alpha * x_ref[...] + h, g_ref[...], b_ref[...])


def _proj_ln(a, x, w, g, b, alpha, tm):
    t, d = x.shape
    k = a.shape[1]
    return pl.pallas_call(
        functools.partial(_proj_ln_kernel, alpha=alpha),
        grid=(t // tm,),
        in_specs=[pl.BlockSpec((tm, k), lambda i: (i, 0)),
                  pl.BlockSpec((tm, d), lambda i: (i, 0)),
                  pl.BlockSpec((k, d), lambda i: (0, 0)),
                  pl.BlockSpec((1, d), lambda i: (0, 0)),
                  pl.BlockSpec((1, d), lambda i: (0, 0))],
        out_specs=pl.BlockSpec((tm, d), lambda i: (i, 0)),
        out_shape=jax.ShapeDtypeStruct((t, d), F32),
        compiler_params=_params("parallel"),
        name="proj_ln",
    )(a, x, w, g, b)


def _cmul(ar, ai, br, bi):
    return ar * br - ai * bi, ar * bi + ai * br


def _s5_tables(a_re, a_im, log_step, b_re, b_im, c_re, c_im, n_levels):
    hi = lax.Precision.HIGHEST
    ch = S5_CHUNK
    g_, p_ = a_re.shape
    n = b_re.shape[-1]
    step = jnp.exp(log_step)[:, None]
    mag = jnp.exp(a_re * step)
    ang = a_im * step
    lam_r = mag * jnp.cos(ang)
    lam_i = mag * jnp.sin(ang)
    den = a_re * a_re + a_im * a_im
    nr = lam_r - 1.0
    ni = lam_i
    f_r = (nr * a_re + ni * a_im) / den
    f_i = (ni * a_re - nr * a_im) / den
    bb_r = f_r[..., None] * b_re - f_i[..., None] * b_im
    bb_i = f_r[..., None] * b_im + f_i[..., None] * b_re
    pr, pi = [jnp.ones_like(lam_r)], [jnp.zeros_like(lam_i)]
    for _ in range(ch):
        r, i = _cmul(pr[-1], pi[-1], lam_r, lam_i)
        pr.append(r)
        pi.append(i)
    pw_r = jnp.stack(pr)
    pw_i = jnp.stack(pi)
    cp_r, cp_i = _cmul(c_re[None], c_im[None], pw_r[:ch, :, None, :], pw_i[:ch, :, None, :])
    kern = (jnp.einsum("dgcp,gpe->dgce", cp_r, bb_r, precision=hi)
            - jnp.einsum("dgcp,gpe->dgce", cp_i, bb_i, precision=hi))
    s_idx = jnp.arange(ch)[:, None]
    t_idx = jnp.arange(ch)[None, :]
    lag = jnp.clip(t_idx - s_idx, 0, ch - 1)
    toep = jnp.where((t_idx >= s_idx)[:, :, None, None, None], kern[lag], 0.0)
    toep = toep.transpose(2, 0, 4, 1, 3).reshape(g_, ch * n, ch * n)
    rev_r = pw_r[ch - 1 - jnp.arange(ch)]
    rev_i = pw_i[ch - 1 - jnp.arange(ch)]
    wi_r, wi_i = _cmul(rev_r[:, :, :, None], rev_i[:, :, :, None], bb_r[None], bb_i[None])
    w_in = jnp.stack([wi_r, wi_i], axis=0).transpose(2, 1, 4, 0, 3).reshape(g_, ch * n, 2 * p_)
    wo_r, wo_i = _cmul(c_re[None], c_im[None], pw_r[1:, :, None, :], pw_i[1:, :, None, :])
    w_out = jnp.stack([wo_r, -wo_i], axis=0).transpose(2, 0, 4, 1, 3).reshape(g_, 2 * p_, ch * n)
    lv = []
    lr, li = pw_r[ch], pw_i[ch]
    for _ in range(n_levels):
        lv.append(jnp.concatenate([lr, lr], axis=-1))
        lv.append(jnp.concatenate([-li, li], axis=-1))
        lr, li = _cmul(lr, li, lr, li)
    lam_lv = jnp.stack(lv, axis=1)
    return toep.astype(BF16), w_in.astype(BF16), w_out.astype(BF16), lam_lv.astype(F32)


def _s5_ssm_kernel(u_ref, toep_ref, win_ref, wout_ref, lam_ref, y_ref):
    u = u_ref[...]
    intra = _dot(u, toep_ref[...])
    h = _dot(u, win_ref[...])
    n_chunks, two_p = h.shape
    row = lax.broadcasted_iota(jnp.int32, h.shape, 0)
    shift, level = 1, 0
    while shift < n_chunks:
        a = lam_ref[2 * level:2 * level + 1, :]
        b = lam_ref[2 * level + 1:2 * level + 2, :]
        hs = jnp.where(row >= shift, pltpu.roll(h, shift, axis=0), 0.0)
        h = h + a * hs + b * pltpu.roll(hs, two_p // 2, axis=1)
        shift *= 2
        level += 1
    h_prev = jnp.where(row >= 1, pltpu.roll(h, 1, axis=0), 0.0)
    y_ref[...] = intra + _dot(h_prev.astype(BF16), wout_ref[...])


def _s5_ssm(u_lay, tables, batch):
    toep, w_in, w_out, lam_lv = tables
    g_, rows, width = u_lay.shape
    nc = rows // batch
    two_p = w_in.shape[-1]
    return pl.pallas_call(
        _s5_ssm_kernel,
        grid=(g_, batch),
        in_specs=[pl.BlockSpec((None, nc, width), lambda g, b: (g, b, 0)),
                  pl.BlockSpec((None, width, width), lambda g, b: (g, 0, 0)),
                  pl.BlockSpec((None, width, two_p), lambda g, b: (g, 0, 0)),
                  pl.BlockSpec((None, two_p, width), lambda g, b: (g, 0, 0)),
                  pl.BlockSpec((None, lam_lv.shape[1], two_p), lambda g, b: (g, 0, 0))],
        out_specs=pl.BlockSpec((None, nc, width), lambda g, b: (g, b, 0)),
        out_shape=jax.ShapeDtypeStruct((g_, rows, width), F32),
        compiler_params=_params("parallel", "parallel"),
        name="s5_ssm",
    )(u_lay, toep, w_in, w_out, lam_lv)


def _s5_out_kernel(y_ref, u_ref, x_ref, d_ref, w_ref, g_ref, b_ref, o_ref, *, alpha):
    d_model = x_ref.shape[1]
    v = _gelu_tanh(y_ref[...] + d_ref[...] * u_ref[...])
    hw = _dot(v.astype(BF16), w_ref[...])
    h = hw[:, :d_model] * _sigmoid(hw[:, d_model:])
    o_ref[...] = _layer_norm(alpha * x_ref[...] + h, g_ref[...], b_ref[...])


def _s5_out(y, u, x, d_skip, w_out, g, b, alpha, tm):
    t, d = x.shape
    tok = pl.BlockSpec((tm, d), lambda i: (i, 0))
    vec = pl.BlockSpec((1, d), lambda i: (0, 0))
    return pl.pallas_call(
        functools.partial(_s5_out_kernel, alpha=alpha),
        grid=(t // tm,),
        in_specs=[tok, tok, tok, vec, pl.BlockSpec((d, 2 * d), lambda i: (0, 0)), vec, vec],
        out_specs=tok,
        out_shape=jax.ShapeDtypeStruct((t, d), F32),
        compiler_params=_params("parallel"),
        name="s5_out",
    )(y, u, x, d_skip, w_out, g, b)


def _s5_layer(x, batch, w_in, a_re, a_im, log_step, b_re, b_im, c_re, c_im, d_skip, w_out, g, b, alpha):
    t, d = x.shape
    seq = t // batch
    g_ = a_re.shape[0]
    n = d // g_
    ch = S5_CHUNK
    n_chunks = seq // ch
    n_levels = max(1, (n_chunks - 1).bit_length())
    tables = _s5_tables(a_re, a_im, log_step, b_re, b_im, c_re, c_im, n_levels)
    u = _matmul(x, w_in.astype(BF16), F32, 512)
    u_lay = (u.astype(BF16).reshape(batch, n_chunks, ch, g_, n)
             .transpose(3, 0, 1, 2, 4).reshape(g_, batch * n_chunks, ch * n))
    y_lay = _s5_ssm(u_lay, tables, batch)
    y = y_lay.reshape(g_, batch, n_chunks, ch, n).transpose(1, 2, 3, 0, 4).reshape(t, d)
    return _s5_out(y, u, x, d_skip.reshape(1, d), w_out.astype(BF16), g, b, alpha, 512)


def _sb_attn_kernel(q_ref, k_ref, v_ref, tri_ref, o_ref, acc_ref, carry_ref, *, blk, head_dim):
    i = pl.program_id(2)
    q = q_ref[...]
    lane = lax.broadcasted_iota(jnp.int32, q.shape, 1)
    zero = jnp.zeros_like(q)
    q_heads = (jnp.where(lane < head_dim, q, zero), jnp.where(lane >= head_dim, q, zero))
    acc_ref[...] = jnp.zeros_like(acc_ref)
    carry_ref[...] = jnp.zeros_like(carry_ref)
    row = i * blk + lax.broadcasted_iota(jnp.int32, (blk, blk), 0)
    col = lax.broadcasted_iota(jnp.int32, (blk, blk), 1)

    def body(state):
        j, _ = state
        start = pl.multiple_of(j * blk, blk)
        kb = k_ref[pl.ds(start, blk), :]
        vb = v_ref[pl.ds(start, blk), :]
        visible = (col + j * blk) < row
        low = jnp.float32(jnp.inf)
        for h in range(2):
            z = _dot_nt(q_heads[h], kb)
            sp = jnp.maximum(z, 0.0) + jnp.log1p(jnp.exp(-jnp.abs(z)))
            sp = jnp.where(visible, sp, 0.0)
            sp_hi = sp.astype(BF16)
            sp_lo = (sp - sp_hi.astype(F32)).astype(BF16)
            sums = _dot(jnp.concatenate([sp_hi, sp_lo], axis=1), tri_ref[...])
            suffix = sums[:, :blk]
            total = sums[:, blk:]
            carry = carry_ref[h]
            w = jnp.where(visible, jnp.exp(z - suffix - carry), 0.0)
            acc_ref[h] += _dot(w.astype(BF16), vb)
            carry = carry + total
            carry_ref[h] = carry
            low = jnp.minimum(low, jnp.min(carry))
        return j - 1, low

    def cond(state):
        j, low = state
        return jnp.logical_and(j >= 0, low < SB_SKIP_LOGIT)

    lax.while_loop(cond, body, (i, jnp.float32(0.0)))
    o_ref[...] = jnp.where(lane < head_dim, acc_ref[0], acc_ref[1]).astype(o_ref.dtype)


def _sb_attention(qkv, batch, d_model):
    t = qkv.shape[0]
    seq = t // batch
    blk = SB_BLOCK
    head_dim = d_model // SB_HEADS
    pairs = d_model // LANES
    qkv3 = qkv.reshape(batch, seq, 3 * d_model)
    r = jnp.arange(blk)
    tri = (r[:, None] >= r[None, :]).astype(BF16)
    tri = jnp.concatenate([tri, jnp.ones((blk, blk), BF16)], axis=1)
    tri = jnp.concatenate([tri, tri], axis=0)
    out = pl.pallas_call(
        functools.partial(_sb_attn_kernel, blk=blk, head_dim=head_dim),
        grid=(batch, pairs, seq // blk),
        in_specs=[pl.BlockSpec((None, blk, LANES), lambda b, p, i: (b, i, p)),
                  pl.BlockSpec((None, seq, LANES), lambda b, p, i: (b, 0, pairs + p)),
                  pl.BlockSpec((None, seq, LANES), lambda b, p, i: (b, 0, 2 * pairs + p)),
                  pl.BlockSpec((2 * blk, 2 * blk), lambda b, p, i: (0, 0))],
        out_specs=pl.BlockSpec((None, blk, LANES), lambda b, p, i: (b, i, p)),
        out_shape=jax.ShapeDtypeStruct((batch, seq, d_model), BF16),
        scratch_shapes=[pltpu.VMEM((2, blk, LANES), F32), pltpu.VMEM((2, blk, blk), F32)],
        compiler_params=_params("parallel", "parallel", "arbitrary"),
        name="sb_attn",
    )(qkv3, qkv3, qkv3, tri)
    return out.reshape(t, d_model)


def _sb_layer(x, batch, w_qkv, w_o, g, b, alpha):
    d = x.shape[1]
    scale = (d // SB_HEADS) ** -0.5
    col_scale = jnp.concatenate([jnp.full((d,), scale, F32), jnp.ones((2 * d,), F32)])
    qkv = _matmul(x, (w_qkv * col_scale).astype(BF16), BF16, 512)
    o = _sb_attention(qkv, batch, d)
    return _proj_ln(o, x, w_o.astype(BF16), g, b, alpha, 512)


def _xa_kernel(x_ref, wq_ref, kv_ref, wo_ref, g_ref, b_ref, o_ref, *, alpha, heads):
    x = x_ref[...]
    d_model = x.shape[1]
    dh = d_model // heads
    q = _dot(x.astype(BF16), wq_ref[...]).astype(BF16)
    outs = []
    for h in range(heads):
        kh = kv_ref[:, h * dh:(h + 1) * dh]
        vh = kv_ref[:, d_model + h * dh:d_model + (h + 1) * dh]
        s = _dot_nt(q[:, h * dh:(h + 1) * dh], kh)
        p = jnp.exp(s - jnp.max(s, axis=-1, keepdims=True))
        p = p / jnp.sum(p, axis=-1, keepdims=True)
        outs.append(_dot(p.astype(BF16), vh).astype(BF16))
    h_out = _dot(jnp.concatenate(outs, axis=1), wo_ref[...])
    o_ref[...] = _layer_norm(alpha * x + h_out, g_ref[...], b_ref[...])


def _xa_layer(x, batch, kv, w_q, w_o, g, b, alpha, tm):
    t, d = x.shape
    seq = t // batch
    mem_len = kv.shape[0] // batch
    scale = (d // XA_HEADS) ** -0.5
    x3 = x.reshape(batch, seq, d)
    kv3 = kv.reshape(batch, mem_len, 2 * d)
    tok = pl.BlockSpec((None, tm, d), lambda bi, i: (bi, i, 0))
    vec = pl.BlockSpec((1, d), lambda bi, i: (0, 0))
    mat = pl.BlockSpec((d, d), lambda bi, i: (0, 0))
    out = pl.pallas_call(
        functools.partial(_xa_kernel, alpha=alpha, heads=XA_HEADS),
        grid=(batch, seq // tm),
        in_specs=[tok, mat, pl.BlockSpec((None, mem_len, 2 * d), lambda bi, i: (bi, 0, 0)), mat, vec, vec],
        out_specs=tok,
        out_shape=jax.ShapeDtypeStruct((batch, seq, d), F32),
        compiler_params=_params("parallel", "parallel"),
        name="xa",
    )(x3, (w_q * scale).astype(BF16), kv3, w_o.astype(BF16), g, b)
    return out.reshape(t, d)


def _ffn_kernel(x_ref, wup_ref, cw_ref, wdn_ref, g_ref, b_ref, o_ref,
                ubuf_ref, tail_ref, acc_ref, *, alpha, tm):
    n_steps = wdn_ref.shape[0]
    pad = 8

    @pl.when(pl.program_id(1) == 0)
    def _():
        tail_ref[...] = jnp.zeros_like(tail_ref)

    xb = x_ref[...].astype(BF16)
    acc_ref[...] = jnp.zeros_like(acc_ref)

    def conv_half(part, c):
        u = _dot(xb, wup_ref[part, c])
        ubuf_ref[part, 0:pad, :] = tail_ref[part, c]
        ubuf_ref[part, pad:pad + tm, :] = u
        tail_ref[part, c] = u[tm - pad:tm, :]
        cw = cw_ref[part, c]
        u1 = ubuf_ref[part, pad - 1:pad - 1 + tm, :]
        u2 = ubuf_ref[part, pad - 2:pad - 2 + tm, :]
        return cw[2:3, :] * u + cw[1:2, :] * u1 + cw[0:1, :] * u2 + cw[3:4, :]

    def body(c, carry):
        a = conv_half(0, c)
        gate = conv_half(1, c)
        act = a * _gelu_tanh(gate)
        acc_ref[...] += _dot(act.astype(BF16), wdn_ref[c])
        return carry

    lax.fori_loop(0, n_steps, body, 0)
    o_ref[...] = _layer_norm(alpha * x_ref[...] + acc_ref[...], g_ref[...], b_ref[...])


def _ffn_layer(x, batch, w_up, conv_w, conv_b, w_down, g, b, alpha, tm):
    t, d = x.shape
    seq = t // batch
    f = w_down.shape[0]
    cols = FFN_COLS
    n_steps = f // cols
    x3 = x.reshape(batch, seq, d)
    wup = w_up.astype(BF16).reshape(d, 2, n_steps, cols).transpose(1, 2, 0, 3)
    cw = jnp.concatenate([conv_w, conv_b[None, :], jnp.zeros((4, 2 * f), F32)], axis=0)
    cw = cw.reshape(8, 2, n_steps, cols).transpose(1, 2, 0, 3)
    wdn = w_down.astype(BF16).reshape(n_steps, cols, d)
    tok = pl.BlockSpec((None, tm, d), lambda bi, i: (bi, i, 0))
    vec = pl.BlockSpec((1, d), lambda bi, i: (0, 0))
    out = pl.pallas_call(
        functools.partial(_ffn_kernel, alpha=alpha, tm=tm),
        grid=(batch, seq // tm),
        in_specs=[tok,
                  pl.BlockSpec((2, n_steps, d, cols), lambda bi, i: (0, 0, 0, 0)),
                  pl.BlockSpec((2, n_steps, 8, cols), lambda bi, i: (0, 0, 0, 0)),
                  pl.BlockSpec((n_steps, cols, d), lambda bi, i: (0, 0, 0)),
                  vec, vec],
        out_specs=tok,
        out_shape=jax.ShapeDtypeStruct((batch, seq, d), F32),
        scratch_shapes=[pltpu.VMEM((2, tm + 8, cols), F32),
                        pltpu.VMEM((2, n_steps, 8, cols), F32),
                        pltpu.VMEM((tm, d), F32)],
        compiler_params=_params("parallel", "arbitrary"),
        name="ffn",
    )(x3, wup, cw, wdn, g, b)
    return out.reshape(t, d)


def kernel(x, mem, s5_w_in, s5_a_re, s5_a_im, s5_log_step, s5_b_re, s5_b_im, s5_c_re, s5_c_im, s5_d, s5_w_out,
           sb_w_qkv, sb_w_o, xa_w_q, xa_w_kv, xa_w_o, ffn_w_up, ffn_conv_w, ffn_conv_b, ffn_w_down, ln_g, ln_b):
    batch, seq, d = x.shape
    depth = ln_g.shape[0]
    alpha = (2.0 * depth) ** 0.25
    mem_len = mem.shape[1]
    xf = x.reshape(batch * seq, d)
    mem_f = mem.reshape(batch * mem_len, d)
    for i in range(depth):
        j = i // 2
        g = ln_g[i].reshape(3, 1, d)
        b = ln_b[i].reshape(3, 1, d)
        if i % 2 == 0:
            xf = _s5_layer(xf, batch, s5_w_in[j], s5_a_re[j], s5_a_im[j], s5_log_step[j], s5_b_re[j], s5_b_im[j],
                           s5_c_re[j], s5_c_im[j], s5_d[j], s5_w_out[j], g[0], b[0], alpha)
        else:
            xf = _sb_layer(xf, batch, sb_w_qkv[j], sb_w_o[j], g[0], b[0], alpha)
        kv = _matmul(mem_f, xa_w_kv[i].astype(BF16), BF16, 512)
        xf = _xa_layer(xf, batch, kv, xa_w_q[i], xa_w_o[i], g[1], b[1], alpha, 512)
        xf = _ffn_layer(xf, batch, ffn_w_up[i], ffn_conv_w[i], ffn_conv_b[i], ffn_w_down[i], g[2], b[2], alpha, 256)
    return xf.reshape(batch, seq, d)
```

```python
import functools

import jax
import jax.numpy as jnp
from jax import lax
from jax.experimental import pallas as pl
from jax.experimental.pallas import tpu as pltpu

F32 = jnp.float32
BF16 = jnp.bfloat16

LN_EPS = 1e-5
LANES = 128
SB_HEADS = 16
XA_HEADS = 4
PROJ_TM = 512
S5_CHUNK = 16
S5_TM = 512
S5_GROUPS_PER_STEP = 4
XA_TM = 512
FFN_COLS = 256
FFN_TM = 512
SB_BLOCK = 128
SB_SKIP_LOGIT = 105.0
SB_MASKED = -1e30
SB_WINDOW = 3
SB_SUB = 2
VMEM_LIMIT = 56 * 1024 * 1024


def _params(*sem):
    return pltpu.CompilerParams(dimension_semantics=sem, vmem_limit_bytes=VMEM_LIMIT)


def _layer_norm(y, g, b):
    mu = jnp.mean(y, axis=-1, keepdims=True)
    yc = y - mu
    var = jnp.mean(yc * yc, axis=-1, keepdims=True)
    return yc * lax.rsqrt(var + LN_EPS) * g + b


def _gelu_tanh(x):
    return 0.5 * x * (1.0 + jnp.tanh(0.7978845608028654 * (x + 0.044715 * (x * x * x))))


def _sigmoid(x):
    return 1.0 / (1.0 + jnp.exp(-x))


def _dot(a, b):
    return jnp.dot(a, b, preferred_element_type=F32)


def _dot_nt(a, b):
    return lax.dot_general(a, b, (((1,), (1,)), ((), ())), preferred_element_type=F32)


def _matmul_kernel(x_ref, w_ref, o_ref):
    o_ref[...] = _dot(x_ref[...].astype(BF16), w_ref[...]).astype(o_ref.dtype)


def _matmul(x, w, out_dtype, tm):
    t, k = x.shape
    n = w.shape[1]
    return pl.pallas_call(
        _matmul_kernel,
        grid=(t // tm,),
        in_specs=[pl.BlockSpec((tm, k), lambda i: (i, 0)),
                  pl.BlockSpec((k, n), lambda i: (0, 0))],
        out_specs=pl.BlockSpec((tm, n), lambda i: (i, 0)),
        out_shape=jax.ShapeDtypeStruct((t, n), out_dtype),
        compiler_params=_params("parallel"),
        name="matmul",
    )(x, w)


def _proj_ln_kernel(a_ref, x_ref, w_ref, g_ref, b_ref, o_ref, *, alpha):
    h = _dot(a_ref[...], w_ref[...])
    o_ref[...] = _layer_norm(alpha * x_ref[...] + h, g_ref[...], b_ref[...])


def _proj_ln(a, x, w, g, b, alpha, tm):
    t, d = x.shape
    k = a.shape[1]
    return pl.pallas_call(
        functools.partial(_proj_ln_kernel, alpha=alpha),
        grid=(t // tm,),
        in_specs=[pl.BlockSpec((tm, k), lambda i: (i, 0)),
                  pl.BlockSpec((tm, d), lambda i: (i, 0)),
                  pl.BlockSpec((k, d), lambda i: (0, 0)),
                  pl.BlockSpec((1, d), lambda i: (0, 0)),
                  pl.BlockSpec((1, d), lambda i: (0, 0))],
        out_specs=pl.BlockSpec((tm, d), lambda i: (i, 0)),
        out_shape=jax.ShapeDtypeStruct((t, d), F32),
        compiler_params=_params("parallel"),
        name="proj_ln",
    )(a, x, w, g, b)


def _cmul(ar, ai, br, bi):
    return ar * br - ai * bi, ar * bi + ai * br


def _s5_tables(a_re, a_im, log_step, b_re, b_im, c_re, c_im, n_levels):
    hi = lax.Precision.HIGHEST
    ch = S5_CHUNK
    g_, p_ = a_re.shape
    n = b_re.shape[-1]
    step = jnp.exp(log_step)[:, None]
    mag = jnp.exp(a_re * step)
    ang = a_im * step
    lam_r = mag * jnp.cos(ang)
    lam_i = mag * jnp.sin(ang)
    den = a_re * a_re + a_im * a_im
    nr = lam_r - 1.0
    ni = lam_i
    f_r = (nr * a_re + ni * a_im) / den
    f_i = (ni * a_re - nr * a_im) / den
    bb_r = f_r[..., None] * b_re - f_i[..., None] * b_im
    bb_i = f_r[..., None] * b_im + f_i[..., None] * b_re
    pr, pi = [jnp.ones_like(lam_r)], [jnp.zeros_like(lam_i)]
    for _ in range(ch):
        r, i = _cmul(pr[-1], pi[-1], lam_r, lam_i)
        pr.append(r)
        pi.append(i)
    pw_r = jnp.stack(pr)
    pw_i = jnp.stack(pi)
    cp_r, cp_i = _cmul(c_re[None], c_im[None], pw_r[:ch, :, None, :], pw_i[:ch, :, None, :])
    kern = (jnp.einsum("dgcp,gpe->dgce", cp_r, bb_r, precision=hi)
            - jnp.einsum("dgcp,gpe->dgce", cp_i, bb_i, precision=hi))
    s_idx = jnp.arange(ch)[:, None]
    t_idx = jnp.arange(ch)[None, :]
    lag = jnp.clip(t_idx - s_idx, 0, ch - 1)
    toep = jnp.where((t_idx >= s_idx)[:, :, None, None, None], kern[lag], 0.0)
    toep = toep.transpose(2, 0, 4, 1, 3).reshape(g_, ch * n, ch * n)
    rev_r = pw_r[ch - 1 - jnp.arange(ch)]
    rev_i = pw_i[ch - 1 - jnp.arange(ch)]
    wi_r, wi_i = _cmul(rev_r[:, :, :, None], rev_i[:, :, :, None], bb_r[None], bb_i[None])
    w_in = jnp.stack([wi_r, wi_i], axis=0).transpose(2, 1, 4, 0, 3).reshape(g_, ch * n, 2 * p_)
    wo_r, wo_i = _cmul(c_re[None], c_im[None], pw_r[1:, :, None, :], pw_i[1:, :, None, :])
    w_out = jnp.stack([wo_r, -wo_i], axis=0).transpose(2, 0, 4, 1, 3).reshape(g_, 2 * p_, ch * n)
    lv = []
    lr, li = pw_r[ch], pw_i[ch]
    for _ in range(n_levels):
        lv.append(jnp.concatenate([lr, lr], axis=-1))
        lv.append(jnp.concatenate([-li, li], axis=-1))
        lr, li = _cmul(lr, li, lr, li)
    lam_lv = jnp.stack(lv, axis=1)
    return toep.astype(BF16), w_in.astype(BF16), w_out.astype(BF16), lam_lv.astype(F32)


def _s5_in_kernel(x_ref, w_ref, u_ref, ulay_ref, us_ref, *, ch, n):
    u = _dot(x_ref[...].astype(BF16), w_ref[...])
    u_ref[...] = u
    tm, d = u.shape
    nk = tm // ch
    per_tile = LANES // n
    for tile in range(d // LANES):
        us_ref[tile] = u[:, tile * LANES:(tile + 1) * LANES]
    slot = lax.broadcasted_iota(jnp.int32, (nk, LANES), 1) // n
    for tile in range(d // LANES):
        phases = [us_ref[tile, pl.ds(s, nk, stride=ch), :] for s in range(ch)]
        for gl in range(per_tile):
            for half in range(ch // per_tile):
                acc = None
                for j in range(per_tile):
                    src = phases[half * per_tile + j]
                    shift = ((j - gl) % per_tile) * n
                    r = pltpu.roll(src, shift, axis=1) if shift else src
                    acc = r if acc is None else jnp.where(slot == j, r, acc)
                ulay_ref[tile * per_tile + gl, :, half * LANES:(half + 1) * LANES] = acc.astype(BF16)


def _s5_in(x, w_in, g_, tm):
    t, d = x.shape
    n = d // g_
    ch = S5_CHUNK
    nk = tm // ch
    return pl.pallas_call(
        functools.partial(_s5_in_kernel, ch=ch, n=n),
        grid=(t // tm,),
        in_specs=[pl.BlockSpec((tm, d), lambda i: (i, 0)),
                  pl.BlockSpec((d, d), lambda i: (0, 0))],
        out_specs=[pl.BlockSpec((tm, d), lambda i: (i, 0)),
                   pl.BlockSpec((g_, nk, ch * n), lambda i: (0, i, 0))],
        out_shape=[jax.ShapeDtypeStruct((t, d), F32),
                   jax.ShapeDtypeStruct((g_, t // ch, ch * n), BF16)],
        scratch_shapes=[pltpu.VMEM((d // LANES, tm, LANES), F32)],
        compiler_params=_params("parallel"),
        name="s5_in",
    )(x, w_in)


def _s5_ssm_kernel(u_ref, toep_ref, win_ref, wout_ref, lam_ref, y_ref):
    n_chunks = u_ref.shape[1]
    two_p = win_ref.shape[2]
    row = lax.broadcasted_iota(jnp.int32, (n_chunks, two_p), 0)
    for gi in range(u_ref.shape[0]):
        u = u_ref[gi]
        intra = _dot(u, toep_ref[gi])
        h = _dot(u, win_ref[gi])
        shift, level = 1, 0
        while shift < n_chunks:
            a = lam_ref[gi, 2 * level:2 * level + 1, :]
            b = lam_ref[gi, 2 * level + 1:2 * level + 2, :]
            hs = jnp.where(row >= shift, pltpu.roll(h, shift, axis=0), 0.0)
            h = h + a * hs + b * pltpu.roll(hs, two_p // 2, axis=1)
            shift *= 2
            level += 1
        h_prev = jnp.where(row >= 1, pltpu.roll(h, 1, axis=0), 0.0)
        y_ref[gi] = intra + _dot(h_prev.astype(BF16), wout_ref[gi])


def _s5_ssm(u_lay, tables, batch):
    toep, w_in, w_out, lam_lv = tables
    g_, rows, width = u_lay.shape
    nc = rows // batch
    two_p = w_in.shape[-1]
    gb = S5_GROUPS_PER_STEP
    return pl.pallas_call(
        _s5_ssm_kernel,
        grid=(g_ // gb, batch),
        in_specs=[pl.BlockSpec((gb, nc, width), lambda g, b: (g, b, 0)),
                  pl.BlockSpec((gb, width, width), lambda g, b: (g, 0, 0)),
                  pl.BlockSpec((gb, width, two_p), lambda g, b: (g, 0, 0)),
                  pl.BlockSpec((gb, two_p, width), lambda g, b: (g, 0, 0)),
                  pl.BlockSpec((gb, lam_lv.shape[1], two_p), lambda g, b: (g, 0, 0))],
        out_specs=pl.BlockSpec((gb, nc, width), lambda g, b: (g, b, 0)),
        out_shape=jax.ShapeDtypeStruct((g_, rows, width), F32),
        compiler_params=_params("parallel", "parallel"),
        name="s5_ssm",
    )(u_lay, toep, w_in, w_out, lam_lv)


def _s5_out_kernel(ylay_ref, u_ref, x_ref, d_ref, w_ref, g_ref, b_ref, o_ref, ys_ref, *, alpha, ch, n):
    tm, d_model = x_ref.shape
    nk = tm // ch
    per_tile = LANES // n
    slot = lax.broadcasted_iota(jnp.int32, (nk, LANES), 1) // n
    for tile in range(d_model // LANES):
        for t in range(ch):
            half, tj = divmod(t, per_tile)
            acc = None
            for gl in range(per_tile):
                src = ylay_ref[tile * per_tile + gl, :, half * LANES:(half + 1) * LANES]
                shift = ((gl - tj) % per_tile) * n
                r = pltpu.roll(src, shift, axis=1) if shift else src
                acc = r if acc is None else jnp.where(slot == gl, r, acc)
            ys_ref[tile, pl.ds(t, nk, stride=ch), :] = acc
    y = jnp.concatenate([ys_ref[tile] for tile in range(d_model // LANES)], axis=1)
    v = _gelu_tanh(y + d_ref[...] * u_ref[...])
    hw = _dot(v.astype(BF16), w_ref[...])
    h = hw[:, :d_model] * _sigmoid(hw[:, d_model:])
    o_ref[...] = _layer_norm(alpha * x_ref[...] + h, g_ref[...], b_ref[...])


def _s5_out(y_lay, u, x, d_skip, w_out, g, b, alpha, tm):
    t, d = x.shape
    g_, _, width = y_lay.shape
    ch = S5_CHUNK
    tok = pl.BlockSpec((tm, d), lambda i: (i, 0))
    vec = pl.BlockSpec((1, d), lambda i: (0, 0))
    return pl.pallas_call(
        functools.partial(_s5_out_kernel, alpha=alpha, ch=ch, n=width // ch),
        grid=(t // tm,),
        in_specs=[pl.BlockSpec((g_, tm // ch, width), lambda i: (0, i, 0)),
                  tok, tok, vec, pl.BlockSpec((d, 2 * d), lambda i: (0, 0)), vec, vec],
        out_specs=tok,
        out_shape=jax.ShapeDtypeStruct((t, d), F32),
        scratch_shapes=[pltpu.VMEM((d // LANES, tm, LANES), F32)],
        compiler_params=_params("parallel"),
        name="s5_out",
    )(y_lay, u, x, d_skip, w_out, g, b)


def _s5_layer(x, batch, w_in, a_re, a_im, log_step, b_re, b_im, c_re, c_im, d_skip, w_out, g, b, alpha):
    t, d = x.shape
    seq = t // batch
    g_ = a_re.shape[0]
    n_chunks = seq // S5_CHUNK
    n_levels = max(1, (n_chunks - 1).bit_length())
    tables = _s5_tables(a_re, a_im, log_step, b_re, b_im, c_re, c_im, n_levels)
    u, u_lay = _s5_in(x, w_in.astype(BF16), g_, S5_TM)
    y_lay = _s5_ssm(u_lay, tables, batch)
    return _s5_out(y_lay, u, x, d_skip.reshape(1, d), w_out.astype(BF16), g, b, alpha, S5_TM)


def _sb_attn_kernel(q_ref, k_ref, v_ref, tri_ref, o_ref, acc_ref, carry_ref, *, blk, head_dim, window, sub):
    seq = q_ref.shape[0]
    lane = lax.broadcasted_iota(jnp.int32, (blk, LANES), 1)
    col2 = lax.broadcasted_iota(jnp.int32, (blk, 2 * blk), 1)
    row2 = lax.broadcasted_iota(jnp.int32, (blk, 2 * blk), 0)
    diag_visible = (col2 & (blk - 1)) < row2

    def load_kv(j):
        start = pl.multiple_of(jnp.maximum(j, 0) * blk, blk)
        return k_ref[pl.ds(start, blk), :], v_ref[pl.ds(start, blk), :]

    def split_heads(kb):
        zero = jnp.zeros_like(kb)
        return jnp.concatenate([jnp.where(lane < head_dim, kb, zero), jnp.where(lane >= head_dim, kb, zero)], axis=0)

    def softplus_split(z):
        sp = jnp.maximum(z, 0.0) + jnp.log(1.0 + jnp.exp(-jnp.abs(z)))
        hi = sp.astype(BF16)
        return hi, (sp - hi.astype(F32)).astype(BF16)

    def suffix_sums(hi, lo):
        return (_dot(jnp.concatenate([hi[:, :blk], lo[:, :blk]], axis=1), tri_ref[...]),
                _dot(jnp.concatenate([hi[:, blk:], lo[:, blk:]], axis=1), tri_ref[...]))

    def both_heads(s0, s1, lo_col):
        return jnp.concatenate([s0[:, lo_col:lo_col + blk], s1[:, lo_col:lo_col + blk]], axis=1)

    def heads_to_rows(w):
        return jnp.concatenate([w[:, :blk], w[:, blk:]], axis=0)

    def near_blocks(it):
        qs = [q_ref[pl.ds(pl.multiple_of((it * sub + s_) * blk, blk), blk), :] for s_ in range(sub)]
        kv = {}
        for off in range(sub - 1, -window, -1):
            j = it * sub + off
            kb, vb = load_kv(j)
            kv[off] = (split_heads(kb), vb, jnp.where(j >= 0, 0.0, SB_MASKED))
        chains = [(s_, d) for s_ in range(sub) for d in range(window)]
        z = {}
        for s_, d in chains:
            k2, _, bias = kv[s_ - d]
            zz = _dot_nt(qs[s_], k2)
            z[s_, d] = jnp.where(diag_visible, zz, SB_MASKED) if d == 0 else zz + bias
        split = {c: softplus_split(z[c]) for c in chains}
        sums = {c: suffix_sums(*split[c]) for c in chains}
        lows = []
        for s_ in range(sub):
            offs = jnp.zeros((blk, 2 * blk), F32)
            ws = []
            for d in range(window):
                s0, s1 = sums[s_, d]
                ws.append(heads_to_rows(jnp.exp(z[s_, d] - both_heads(s0, s1, 0) - offs).astype(BF16)))
                offs = offs + both_heads(s0, s1, blk)
            vcat = jnp.concatenate([kv[s_ - d][1] for d in range(window)], axis=0)
            acc_ref[s_] = _dot(jnp.concatenate(ws, axis=1), vcat)
            carry_ref[s_] = offs
            lows.append(offs)
        return qs, [jnp.min(o) for o in lows]

    def far_blocks(ib, slot, q, low):
        def cond(state):
            j, lo_ = state
            return jnp.logical_and(j >= 0, lo_ < SB_SKIP_LOGIT)

        def body(state):
            j, _ = state
            kb, vb = load_kv(j)
            z = _dot_nt(q, split_heads(kb))
            s0, s1 = suffix_sums(*softplus_split(z))
            carry = carry_ref[slot]
            w = jnp.exp(z - both_heads(s0, s1, 0) - carry).astype(BF16)
            acc_ref[slot] += _dot(heads_to_rows(w), vb)
            carry = carry + both_heads(s0, s1, blk)
            carry_ref[slot] = carry
            return j - 1, jnp.min(carry)

        lax.while_loop(cond, body, (ib - window, low))

    def step(it, carry):
        qs, lows = near_blocks(it)
        for s_ in range(sub):
            far_blocks(it * sub + s_, s_, qs[s_], lows[s_])
        for s_ in range(sub):
            acc = acc_ref[s_]
            r0 = pl.multiple_of((it * sub + s_) * blk, blk)
            o_ref[pl.ds(r0, blk), :] = jnp.where(lane < head_dim, acc[:blk], acc[blk:]).astype(o_ref.dtype)
        return carry

    lax.fori_loop(0, seq // (blk * sub), step, 0)


def _sb_attention(qkv, batch, d_model):
    t = qkv.shape[0]
    seq = t // batch
    blk = SB_BLOCK
    head_dim = d_model // SB_HEADS
    pairs = d_model // LANES
    qkv3 = qkv.reshape(batch, seq, 3 * d_model)
    r = jnp.arange(blk)
    tri = (r[:, None] >= r[None, :]).astype(BF16)
    tri = jnp.concatenate([tri, jnp.ones((blk, blk), BF16)], axis=1)
    tri = jnp.concatenate([tri, tri], axis=0)
    out = pl.pallas_call(
        functools.partial(_sb_attn_kernel, blk=blk, head_dim=head_dim, window=SB_WINDOW, sub=SB_SUB),
        grid=(batch, pairs),
        in_specs=[pl.BlockSpec((None, seq, LANES), lambda b, p: (b, 0, p)),
                  pl.BlockSpec((None, seq, LANES), lambda b, p: (b, 0, pairs + p)),
                  pl.BlockSpec((None, seq, LANES), lambda b, p: (b, 0, 2 * pairs + p)),
                  pl.BlockSpec((2 * blk, 2 * blk), lambda b, p: (0, 0))],
        out_specs=pl.BlockSpec((None, seq, LANES), lambda b, p: (b, 0, p)),
        out_shape=jax.ShapeDtypeStruct((batch, seq, d_model), BF16),
        scratch_shapes=[pltpu.VMEM((SB_SUB, 2 * blk, LANES), F32), pltpu.VMEM((SB_SUB, blk, 2 * blk), F32)],
        compiler_params=_params("parallel", "parallel"),
        name="sb_attn",
    )(qkv3, qkv3, qkv3, tri)
    return out.reshape(t, d_model)


def _sb_layer(x, batch, w_qkv, w_o, g, b, alpha):
    d = x.shape[1]
    scale = (d // SB_HEADS) ** -0.5
    col_scale = jnp.concatenate([jnp.full((d,), scale, F32), jnp.ones((2 * d,), F32)])
    qkv = _matmul(x, (w_qkv * col_scale).astype(BF16), BF16, PROJ_TM)
    o = _sb_attention(qkv, batch, d)
    return _proj_ln(o, x, w_o.astype(BF16), g, b, alpha, PROJ_TM)


def _xa_kernel(x_ref, wq_ref, kv_ref, wo_ref, g_ref, b_ref, o_ref, *, alpha, heads):
    x = x_ref[...]
    d_model = x.shape[1]
    dh = d_model // heads
    q = _dot(x.astype(BF16), wq_ref[...]).astype(BF16)
    outs = []
    for h in range(heads):
        kh = kv_ref[:, h * dh:(h + 1) * dh]
        vh = kv_ref[:, d_model + h * dh:d_model + (h + 1) * dh]
        s = _dot_nt(q[:, h * dh:(h + 1) * dh], kh)
        p = jnp.exp(s - jnp.max(s, axis=-1, keepdims=True))
        p = p / jnp.sum(p, axis=-1, keepdims=True)
        outs.append(_dot(p.astype(BF16), vh).astype(BF16))
    h_out = _dot(jnp.concatenate(outs, axis=1), wo_ref[...])
    o_ref[...] = _layer_norm(alpha * x + h_out, g_ref[...], b_ref[...])


def _xa_layer(x, batch, kv, w_q, w_o, g, b, alpha, tm):
    t, d = x.shape
    seq = t // batch
    mem_len = kv.shape[0] // batch
    scale = (d // XA_HEADS) ** -0.5
    x3 = x.reshape(batch, seq, d)
    kv3 = kv.reshape(batch, mem_len, 2 * d)
    tok = pl.BlockSpec((None, tm, d), lambda bi, i: (bi, i, 0))
    vec = pl.BlockSpec((1, d), lambda bi, i: (0, 0))
    mat = pl.BlockSpec((d, d), lambda bi, i: (0, 0))
    out = pl.pallas_call(
        functools.partial(_xa_kernel, alpha=alpha, heads=XA_HEADS),
        grid=(batch, seq // tm),
        in_specs=[tok, mat, pl.BlockSpec((None, mem_len, 2 * d), lambda bi, i: (bi, 0, 0)), mat, vec, vec],
        out_specs=tok,
        out_shape=jax.ShapeDtypeStruct((batch, seq, d), F32),
        compiler_params=_params("parallel", "parallel"),
        name="xa",
    )(x3, (w_q * scale).astype(BF16), kv3, w_o.astype(BF16), g, b)
    return out.reshape(t, d)


def _ffn_kernel(x_ref, wup_ref, cw_ref, wdn_ref, g_ref, b_ref, o_ref, ubuf_ref, tail_ref, act_ref, *, alpha, cols):
    tm = x_ref.shape[0]
    f = wdn_ref.shape[0]
    pad = tail_ref.shape[0]

    @pl.when(pl.program_id(1) == 0)
    def _():
        tail_ref[...] = jnp.zeros_like(tail_ref)

    xb = x_ref[...].astype(BF16)

    def conv_cols(part, lo):
        sl = slice(lo, lo + cols)
        u = _dot(xb, wup_ref[:, sl])
        ubuf_ref[part, 0:pad, :] = tail_ref[:, sl]
        ubuf_ref[part, pad:pad + tm, :] = u
        tail_ref[:, sl] = u[tm - pad:tm, :]
        u1 = ubuf_ref[part, pad - 1:pad - 1 + tm, :]
        u2 = ubuf_ref[part, pad - 2:pad - 2 + tm, :]
        return cw_ref[2:3, sl] * u + cw_ref[1:2, sl] * u1 + cw_ref[0:1, sl] * u2 + cw_ref[3:4, sl]

    for lo in range(0, f, cols):
        a = conv_cols(0, lo)
        gate = conv_cols(1, f + lo)
        act_ref[:, lo:lo + cols] = (a * _gelu_tanh(gate)).astype(BF16)
    h = _dot(act_ref[...], wdn_ref[...])
    o_ref[...] = _layer_norm(alpha * x_ref[...] + h, g_ref[...], b_ref[...])


def _ffn_layer(x, batch, w_up, conv_w, conv_b, w_down, g, b, alpha, tm):
    t, d = x.shape
    seq = t // batch
    f = w_down.shape[0]
    x3 = x.reshape(batch, seq, d)
    cw = jnp.concatenate([conv_w, conv_b[None, :], jnp.zeros((4, 2 * f), F32)], axis=0)
    tok = pl.BlockSpec((None, tm, d), lambda bi, i: (bi, i, 0))
    vec = pl.BlockSpec((1, d), lambda bi, i: (0, 0))
    out = pl.pallas_call(
        functools.partial(_ffn_kernel, alpha=alpha, cols=FFN_COLS),
        grid=(batch, seq // tm),
        in_specs=[tok,
                  pl.BlockSpec((d, 2 * f), lambda bi, i: (0, 0)),
                  pl.BlockSpec((8, 2 * f), lambda bi, i: (0, 0)),
                  pl.BlockSpec((f, d), lambda bi, i: (0, 0)),
                  vec, vec],
        out_specs=tok,
        out_shape=jax.ShapeDtypeStruct((batch, seq, d), F32),
        scratch_shapes=[pltpu.VMEM((2, tm + 8, FFN_COLS), F32),
                        pltpu.VMEM((8, 2 * f), F32),
                        pltpu.VMEM((tm, f), BF16)],
        compiler_params=_params("parallel", "arbitrary"),
        name="ffn",
    )(x3, w_up.astype(BF16), cw, w_down.astype(BF16), g, b)
    return out.reshape(t, d)


def kernel(x, mem, s5_w_in, s5_a_re, s5_a_im, s5_log_step, s5_b_re, s5_b_im, s5_c_re, s5_c_im, s5_d, s5_w_out,
           sb_w_qkv, sb_w_o, xa_w_q, xa_w_kv, xa_w_o, ffn_w_up, ffn_conv_w, ffn_conv_b, ffn_w_down, ln_g, ln_b):
    batch, seq, d = x.shape
    depth = ln_g.shape[0]
    alpha = (2.0 * depth) ** 0.25
    mem_len = mem.shape[1]
    xf = x.reshape(batch * seq, d)
    mem_f = mem.reshape(batch * mem_len, d)
    for i in range(depth):
        j = i // 2
        g = ln_g[i].reshape(3, 1, d)
        b = ln_b[i].reshape(3, 1, d)
        if i % 2 == 0:
            xf = _s5_layer(xf, batch, s5_w_in[j], s5_a_re[j], s5_a_im[j], s5_log_step[j], s5_b_re[j], s5_b_im[j],
                           s5_c_re[j], s5_c_im[j], s5_d[j], s5_w_out[j], g[0], b[0], alpha)
        else:
            xf = _sb_layer(xf, batch, sb_w_qkv[j], sb_w_o[j], g[0], b[0], alpha)
        kv = _matmul(mem_f, xa_w_kv[i].astype(BF16), BF16, PROJ_TM)
        xf = _xa_layer(xf, batch, kv, xa_w_q[i], xa_w_o[i], g[1], b[1], alpha, XA_TM)
        xf = _ffn_layer(xf, batch, ffn_w_up[i], ffn_conv_w[i], ffn_conv_b[i], ffn_w_down[i], g[2], b[2], alpha, FFN_TM)
    return xf.reshape(batch, seq, d)
```

```python
import functools

import jax
import jax.numpy as jnp
from jax import lax
from jax.experimental import pallas as pl
from jax.experimental.pallas import tpu as pltpu

F32 = jnp.float32
BF16 = jnp.bfloat16

LN_EPS = 1e-5
LANES = 128
SB_HEADS = 16
XA_HEADS = 4
PROJ_TM = 512
S5_CHUNK = 16
S5_TM = 512
S5_GROUPS_PER_STEP = 4
XA_TM = 512
FFN_COLS = 256
FFN_TM = 512
SB_SKIP_LOGIT = 105.0
SB_MASKED = -1e30
SB_WINDOW = 4
SB_SUB = 16
VMEM_LIMIT = 56 * 1024 * 1024


def _params(*sem):
    return pltpu.CompilerParams(dimension_semantics=sem, vmem_limit_bytes=VMEM_LIMIT)


def _layer_norm(y, g, b):
    mu = jnp.mean(y, axis=-1, keepdims=True)
    yc = y - mu
    var = jnp.mean(yc * yc, axis=-1, keepdims=True)
    return yc * lax.rsqrt(var + LN_EPS) * g + b


def _gelu_tanh(x):
    return 0.5 * x * (1.0 + jnp.tanh(0.7978845608028654 * (x + 0.044715 * (x * x * x))))


def _sigmoid(x):
    return 1.0 / (1.0 + jnp.exp(-x))


def _dot(a, b):
    return jnp.dot(a, b, preferred_element_type=F32)


def _dot_nt(a, b):
    return lax.dot_general(a, b, (((1,), (1,)), ((), ())), preferred_element_type=F32)


def _matmul_kernel(x_ref, w_ref, o_ref):
    o_ref[...] = _dot(x_ref[...].astype(BF16), w_ref[...]).astype(o_ref.dtype)


def _matmul(x, w, out_dtype, tm):
    t, k = x.shape
    n = w.shape[1]
    return pl.pallas_call(
        _matmul_kernel,
        grid=(t // tm,),
        in_specs=[pl.BlockSpec((tm, k), lambda i: (i, 0)),
                  pl.BlockSpec((k, n), lambda i: (0, 0))],
        out_specs=pl.BlockSpec((tm, n), lambda i: (i, 0)),
        out_shape=jax.ShapeDtypeStruct((t, n), out_dtype),
        compiler_params=_params("parallel"),
        name="matmul",
    )(x, w)


def _cmul(ar, ai, br, bi):
    return ar * br - ai * bi, ar * bi + ai * br


def _s5_tables(a_re, a_im, log_step, b_re, b_im, c_re, c_im, n_levels):
    hi = lax.Precision.HIGHEST
    ch = S5_CHUNK
    g_, p_ = a_re.shape
    n = b_re.shape[-1]
    step = jnp.exp(log_step)[:, None]
    mag = jnp.exp(a_re * step)
    ang = a_im * step
    lam_r = mag * jnp.cos(ang)
    lam_i = mag * jnp.sin(ang)
    den = a_re * a_re + a_im * a_im
    nr = lam_r - 1.0
    ni = lam_i
    f_r = (nr * a_re + ni * a_im) / den
    f_i = (ni * a_re - nr * a_im) / den
    bb_r = f_r[..., None] * b_re - f_i[..., None] * b_im
    bb_i = f_r[..., None] * b_im + f_i[..., None] * b_re
    pr, pi = [jnp.ones_like(lam_r)], [jnp.zeros_like(lam_i)]
    for _ in range(ch):
        r, i = _cmul(pr[-1], pi[-1], lam_r, lam_i)
        pr.append(r)
        pi.append(i)
    pw_r = jnp.stack(pr)
    pw_i = jnp.stack(pi)
    cp_r, cp_i = _cmul(c_re[None], c_im[None], pw_r[:ch, :, None, :], pw_i[:ch, :, None, :])
    kern = (jnp.einsum("dgcp,gpe->dgce", cp_r, bb_r, precision=hi)
            - jnp.einsum("dgcp,gpe->dgce", cp_i, bb_i, precision=hi))
    s_idx = jnp.arange(ch)[:, None]
    t_idx = jnp.arange(ch)[None, :]
    lag = jnp.clip(t_idx - s_idx, 0, ch - 1)
    toep = jnp.where((t_idx >= s_idx)[:, :, None, None, None], kern[lag], 0.0)
    toep = toep.transpose(2, 0, 4, 1, 3).reshape(g_, ch * n, ch * n)
    rev_r = pw_r[ch - 1 - jnp.arange(ch)]
    rev_i = pw_i[ch - 1 - jnp.arange(ch)]
    wi_r, wi_i = _cmul(rev_r[:, :, :, None], rev_i[:, :, :, None], bb_r[None], bb_i[None])
    w_in = jnp.stack([wi_r, wi_i], axis=0).transpose(2, 1, 4, 0, 3).reshape(g_, ch * n, 2 * p_)
    wo_r, wo_i = _cmul(c_re[None], c_im[None], pw_r[1:, :, None, :], pw_i[1:, :, None, :])
    w_out = jnp.stack([wo_r, -wo_i], axis=0).transpose(2, 0, 4, 1, 3).reshape(g_, 2 * p_, ch * n)
    lv = []
    lr, li = pw_r[ch], pw_i[ch]
    for _ in range(n_levels):
        lv.append(jnp.concatenate([lr, lr], axis=-1))
        lv.append(jnp.concatenate([-li, li], axis=-1))
        lr, li = _cmul(lr, li, lr, li)
    lam_lv = jnp.stack(lv, axis=1)
    return toep.astype(BF16), w_in.astype(BF16), w_out.astype(BF16), lam_lv.astype(F32)


def _s5_in_kernel(x_ref, w_ref, u_ref, ulay_ref, us_ref, *, ch, n):
    u = _dot(x_ref[...].astype(BF16), w_ref[...])
    u_ref[...] = u
    tm, d = u.shape
    nk = tm // ch
    per_tile = LANES // n
    for tile in range(d // LANES):
        us_ref[tile] = u[:, tile * LANES:(tile + 1) * LANES]
    slot = lax.broadcasted_iota(jnp.int32, (nk, LANES), 1) // n
    for tile in range(d // LANES):
        phases = [us_ref[tile, pl.ds(s, nk, stride=ch), :] for s in range(ch)]
        for gl in range(per_tile):
            for half in range(ch // per_tile):
                acc = None
                for j in range(per_tile):
                    src = phases[half * per_tile + j]
                    shift = ((j - gl) % per_tile) * n
                    r = pltpu.roll(src, shift, axis=1) if shift else src
                    acc = r if acc is None else jnp.where(slot == j, r, acc)
                ulay_ref[tile * per_tile + gl, :, half * LANES:(half + 1) * LANES] = acc.astype(BF16)


def _s5_in(x, w_in, g_, tm):
    t, d = x.shape
    n = d // g_
    ch = S5_CHUNK
    nk = tm // ch
    return pl.pallas_call(
        functools.partial(_s5_in_kernel, ch=ch, n=n),
        grid=(t // tm,),
        in_specs=[pl.BlockSpec((tm, d), lambda i: (i, 0)),
                  pl.BlockSpec((d, d), lambda i: (0, 0))],
        out_specs=[pl.BlockSpec((tm, d), lambda i: (i, 0)),
                   pl.BlockSpec((g_, nk, ch * n), lambda i: (0, i, 0))],
        out_shape=[jax.ShapeDtypeStruct((t, d), F32),
                   jax.ShapeDtypeStruct((g_, t // ch, ch * n), BF16)],
        scratch_shapes=[pltpu.VMEM((d // LANES, tm, LANES), F32)],
        compiler_params=_params("parallel"),
        name="s5_in",
    )(x, w_in)


def _s5_ssm_kernel(u_ref, toep_ref, win_ref, wout_ref, lam_ref, y_ref):
    n_chunks = u_ref.shape[1]
    two_p = win_ref.shape[2]
    row = lax.broadcasted_iota(jnp.int32, (n_chunks, two_p), 0)
    for gi in range(u_ref.shape[0]):
        u = u_ref[gi]
        intra = _dot(u, toep_ref[gi])
        h = _dot(u, win_ref[gi])
        shift, level = 1, 0
        while shift < n_chunks:
            a = lam_ref[gi, 2 * level:2 * level + 1, :]
            b = lam_ref[gi, 2 * level + 1:2 * level + 2, :]
            hs = jnp.where(row >= shift, pltpu.roll(h, shift, axis=0), 0.0)
            h = h + a * hs + b * pltpu.roll(hs, two_p // 2, axis=1)
            shift *= 2
            level += 1
        h_prev = jnp.where(row >= 1, pltpu.roll(h, 1, axis=0), 0.0)
        y_ref[gi] = intra + _dot(h_prev.astype(BF16), wout_ref[gi])


def _s5_ssm(u_lay, tables, batch):
    toep, w_in, w_out, lam_lv = tables
    g_, rows, width = u_lay.shape
    nc = rows // batch
    two_p = w_in.shape[-1]
    gb = S5_GROUPS_PER_STEP
    return pl.pallas_call(
        _s5_ssm_kernel,
        grid=(g_ // gb, batch),
        in_specs=[pl.BlockSpec((gb, nc, width), lambda g, b: (g, b, 0)),
                  pl.BlockSpec((gb, width, width), lambda g, b: (g, 0, 0)),
                  pl.BlockSpec((gb, width, two_p), lambda g, b: (g, 0, 0)),
                  pl.BlockSpec((gb, two_p, width), lambda g, b: (g, 0, 0)),
                  pl.BlockSpec((gb, lam_lv.shape[1], two_p), lambda g, b: (g, 0, 0))],
        out_specs=pl.BlockSpec((gb, nc, width), lambda g, b: (g, b, 0)),
        out_shape=jax.ShapeDtypeStruct((g_, rows, width), F32),
        compiler_params=_params("parallel", "parallel"),
        name="s5_ssm",
    )(u_lay, toep, w_in, w_out, lam_lv)


def _s5_mix(ylay_ref, u_ref, x_ref, d_ref, w_ref, g_ref, b_ref, ys_ref, *, alpha, ch, n):
    tm, d_model = x_ref.shape
    nk = tm // ch
    per_tile = LANES // n
    slot = lax.broadcasted_iota(jnp.int32, (nk, LANES), 1) // n
    for tile in range(d_model // LANES):
        for t in range(ch):
            half, tj = divmod(t, per_tile)
            acc = None
            for gl in range(per_tile):
                src = ylay_ref[tile * per_tile + gl, :, half * LANES:(half + 1) * LANES]
                shift = ((gl - tj) % per_tile) * n
                r = pltpu.roll(src, shift, axis=1) if shift else src
                acc = r if acc is None else jnp.where(slot == gl, r, acc)
            ys_ref[tile, pl.ds(t, nk, stride=ch), :] = acc
    y = jnp.concatenate([ys_ref[tile] for tile in range(d_model // LANES)], axis=1)
    v = _gelu_tanh(y + d_ref[...] * u_ref[...])
    hw = _dot(v.astype(BF16), w_ref[...])
    h = hw[:, :d_model] * _sigmoid(hw[:, d_model:])
    return _layer_norm(alpha * x_ref[...] + h, g_ref[...], b_ref[...])


def _s5_front(x, batch, w_in, a_re, a_im, log_step, b_re, b_im, c_re, c_im, d_skip, w_out, g, b):
    t, d = x.shape
    seq = t // batch
    g_ = a_re.shape[0]
    n_chunks = seq // S5_CHUNK
    n_levels = max(1, (n_chunks - 1).bit_length())
    tables = _s5_tables(a_re, a_im, log_step, b_re, b_im, c_re, c_im, n_levels)
    u, u_lay = _s5_in(x, w_in.astype(BF16), g_, S5_TM)
    y_lay = _s5_ssm(u_lay, tables, batch)
    return ("s5", y_lay, u, d_skip.reshape(1, d), w_out.astype(BF16), g, b)


def _sb_attn_kernel(q_ref, k_ref, v_ref, tri_ref, o_ref, acc_ref, carry_ref, *, blk, window, sub):
    seq = q_ref.shape[0]
    lane = lax.broadcasted_iota(jnp.int32, (blk, LANES), 1)
    row = lax.broadcasted_iota(jnp.int32, (blk, LANES), 0)
    diag_visible = (lane & (blk - 1)) < row
    lane2 = lax.broadcasted_iota(jnp.int32, (2 * blk, LANES), 1)
    row2 = lax.broadcasted_iota(jnp.int32, (2 * blk, LANES), 0)
    own_head = (lane2 < blk) == (row2 < blk)

    def load_kv(j):
        start = pl.multiple_of(jnp.maximum(j, 0) * blk, blk)
        return k_ref[pl.ds(start, blk), :], v_ref[pl.ds(start, blk), :]

    def per_head_rows(x):
        x2 = jnp.concatenate([x, x], axis=0)
        return jnp.where(own_head, x2, jnp.zeros_like(x2))

    def softplus_split(z2):
        sp = jnp.maximum(z2, 0.0) + jnp.log(1.0 + jnp.exp(-jnp.abs(z2)))
        hi = sp.astype(BF16)
        return jnp.concatenate([hi, (sp - hi.astype(F32)).astype(BF16)], axis=1)

    def near_blocks(it):
        base = it * sub
        z, vals = {}, {}
        for off in range(1 - window, sub):
            s_lo, s_hi = max(off, 0), min(off + window - 1, sub - 1)
            j = base + off
            kb, vb = load_kv(j)
            vals[off] = per_head_rows(vb)
            rows = pl.ds(pl.multiple_of((base + s_lo) * blk, blk), (s_hi - s_lo + 1) * blk)
            zz = _dot_nt(q_ref[rows, :], per_head_rows(kb))
            if off < 0:
                zz = zz + jnp.where(j >= 0, 0.0, SB_MASKED)
            for s_ in range(s_lo, s_hi + 1):
                piece = zz[(s_ - s_lo) * blk:(s_ - s_lo + 1) * blk]
                z[s_, s_ - off] = jnp.where(diag_visible, piece, SB_MASKED) if s_ == off else piece
        chains = [(s_, d) for s_ in range(sub) for d in range(window)]
        sums = _dot(jnp.concatenate([softplus_split(z[c]) for c in chains], axis=0), tri_ref[...])
        lows = []
        for s_ in range(sub):
            offs = jnp.zeros((blk, LANES), F32)
            ws = []
            for d in range(window):
                sm = sums[(s_ * window + d) * blk:(s_ * window + d + 1) * blk]
                ws.append(jnp.exp(z[s_, d] - sm[:, :LANES] - offs).astype(BF16))
                offs = offs + sm[:, LANES:]
            vcat = jnp.concatenate([vals[s_ - d] for d in range(window)], axis=0)
            acc_ref[s_] = _dot(jnp.concatenate(ws, axis=1), vcat)
            carry_ref[s_] = offs
            lows.append(offs)
        return [jnp.min(o) for o in lows]

    def far_blocks(ib, slot, low):
        def cond(state):
            j, lo_ = state
            return jnp.logical_and(j >= 0, lo_ < SB_SKIP_LOGIT)

        def body(state):
            j, _ = state
            kb, vb = load_kv(j)
            q = q_ref[pl.ds(pl.multiple_of(ib * blk, blk), blk), :]
            z2 = _dot_nt(q, per_head_rows(kb))
            sm = _dot(softplus_split(z2), tri_ref[...])
            carry = carry_ref[slot]
            w = jnp.exp(z2 - sm[:, :LANES] - carry).astype(BF16)
            acc_ref[slot] += _dot(w, per_head_rows(vb))
            carry = carry + sm[:, LANES:]
            carry_ref[slot] = carry
            return j - 1, jnp.min(carry)

        lax.while_loop(cond, body, (ib - window, low))

    def step(it, carry):
        lows = near_blocks(it)
        for s_ in range(sub):
            far_blocks(it * sub + s_, s_, lows[s_])
        for s_ in range(sub):
            r0 = pl.multiple_of((it * sub + s_) * blk, blk)
            o_ref[pl.ds(r0, blk), :] = acc_ref[s_].astype(o_ref.dtype)
        return carry

    lax.fori_loop(0, seq // (blk * sub), step, 0)


def _sb_attention(qkv, batch, d_model):
    t = qkv.shape[0]
    seq = t // batch
    blk = d_model // SB_HEADS
    pairs = d_model // LANES
    qkv3 = qkv.reshape(batch, seq, 3 * d_model)
    r = jnp.arange(2 * blk)
    same_head = (r[:, None] // blk) == (r[None, :] // blk)
    suffix = same_head & (r[:, None] >= r[None, :])
    tri = jnp.concatenate([suffix, same_head], axis=1).astype(BF16)
    tri = jnp.concatenate([tri, tri], axis=0)
    out = pl.pallas_call(
        functools.partial(_sb_attn_kernel, blk=blk, window=SB_WINDOW, sub=SB_SUB),
        grid=(batch, pairs),
        in_specs=[pl.BlockSpec((None, seq, LANES), lambda b, p: (b, 0, p)),
                  pl.BlockSpec((None, seq, LANES), lambda b, p: (b, 0, pairs + p)),
                  pl.BlockSpec((None, seq, LANES), lambda b, p: (b, 0, 2 * pairs + p)),
                  pl.BlockSpec((4 * blk, 4 * blk), lambda b, p: (0, 0))],
        out_specs=pl.BlockSpec((None, seq, LANES), lambda b, p: (b, 0, p)),
        out_shape=jax.ShapeDtypeStruct((batch, seq, d_model), BF16),
        scratch_shapes=[pltpu.VMEM((SB_SUB, blk, LANES), F32), pltpu.VMEM((SB_SUB, blk, LANES), F32)],
        compiler_params=_params("parallel", "parallel"),
        name="sb_attn",
    )(qkv3, qkv3, qkv3, tri)
    return out.reshape(t, d_model)


def _sb_front(x, batch, w_qkv, w_o, g, b):
    d = x.shape[1]
    scale = (d // SB_HEADS) ** -0.5
    col_scale = jnp.concatenate([jnp.full((d,), scale, F32), jnp.ones((2 * d,), F32)])
    qkv = _matmul(x, (w_qkv * col_scale).astype(BF16), BF16, PROJ_TM)
    return ("sb", _sb_attention(qkv, batch, d), w_o.astype(BF16), g, b)


def _xa_core(x, wq_ref, kv_ref, wo_ref, g_ref, b_ref, o_ref, *, alpha, heads):
    d_model = x.shape[1]
    dh = d_model // heads
    q = _dot(x.astype(BF16), wq_ref[...]).astype(BF16)
    outs = []
    for h in range(heads):
        kh = kv_ref[:, h * dh:(h + 1) * dh]
        vh = kv_ref[:, d_model + h * dh:d_model + (h + 1) * dh]
        s = _dot_nt(q[:, h * dh:(h + 1) * dh], kh)
        p = jnp.exp(s - jnp.max(s, axis=-1, keepdims=True))
        p = p / jnp.sum(p, axis=-1, keepdims=True)
        outs.append(_dot(p.astype(BF16), vh).astype(BF16))
    h_out = _dot(jnp.concatenate(outs, axis=1), wo_ref[...])
    o_ref[...] = _layer_norm(alpha * x + h_out, g_ref[...], b_ref[...])


def _sb_xa_kernel(a_ref, x_ref, wpre_ref, gpre_ref, bpre_ref, wq_ref, kv_ref, wo_ref, g_ref, b_ref, o_ref,
                  *, alpha, heads):
    x = _layer_norm(alpha * x_ref[...] + _dot(a_ref[...], wpre_ref[...]), gpre_ref[...], bpre_ref[...])
    _xa_core(x, wq_ref, kv_ref, wo_ref, g_ref, b_ref, o_ref, alpha=alpha, heads=heads)


def _s5_xa_kernel(ylay_ref, u_ref, x_ref, d_ref, wpre_ref, gpre_ref, bpre_ref, wq_ref, kv_ref, wo_ref, g_ref, b_ref,
                  o_ref, ys_ref, *, alpha, heads, ch, n):
    x = _s5_mix(ylay_ref, u_ref, x_ref, d_ref, wpre_ref, gpre_ref, bpre_ref, ys_ref, alpha=alpha, ch=ch, n=n)
    _xa_core(x, wq_ref, kv_ref, wo_ref, g_ref, b_ref, o_ref, alpha=alpha, heads=heads)


def _mixer_xa(front, x, batch, kv, w_q, w_o, g, b, alpha, tm):
    t, d = x.shape
    seq = t // batch
    tiles = seq // tm
    mem_len = kv.shape[0] // batch
    scale = (d // XA_HEADS) ** -0.5
    tok = pl.BlockSpec((tm, d), lambda bi, i: (bi * tiles + i, 0))
    vec = pl.BlockSpec((1, d), lambda bi, i: (0, 0))
    mat = pl.BlockSpec((d, d), lambda bi, i: (0, 0))
    xa_specs = [mat, pl.BlockSpec((None, mem_len, 2 * d), lambda bi, i: (bi, 0, 0)), mat, vec, vec]
    xa_args = ((w_q * scale).astype(BF16), kv.reshape(batch, mem_len, 2 * d), w_o.astype(BF16), g, b)
    if front[0] == "sb":
        _, a, w_pre, g0, b0 = front
        body = functools.partial(_sb_xa_kernel, alpha=alpha, heads=XA_HEADS)
        specs = [tok, tok, mat, vec, vec]
        args = (a, x, w_pre, g0, b0)
        scratch = []
    else:
        _, y_lay, u, d_skip, w_pre, g0, b0 = front
        g_, _, width = y_lay.shape
        ch = S5_CHUNK
        body = functools.partial(_s5_xa_kernel, alpha=alpha, heads=XA_HEADS, ch=ch, n=width // ch)
        specs = [pl.BlockSpec((g_, tm // ch, width), lambda bi, i: (0, bi * tiles + i, 0)),
                 tok, tok, vec, pl.BlockSpec((d, 2 * d), lambda bi, i: (0, 0)), vec, vec]
        args = (y_lay, u, x, d_skip, w_pre, g0, b0)
        scratch = [pltpu.VMEM((d // LANES, tm, LANES), F32)]
    return pl.pallas_call(
        body,
        grid=(batch, tiles),
        in_specs=specs + xa_specs,
        out_specs=tok,
        out_shape=jax.ShapeDtypeStruct((t, d), F32),
        scratch_shapes=scratch,
        compiler_params=_params("parallel", "parallel"),
        name=front[0] + "_xa",
    )(*args, *xa_args)


def _ffn_kernel(x_ref, wup_ref, cw_ref, wdn_ref, g_ref, b_ref, o_ref, ubuf_ref, tail_ref, act_ref, *, alpha, cols):
    tm = x_ref.shape[0]
    f = wdn_ref.shape[0]
    pad = tail_ref.shape[0]

    @pl.when(pl.program_id(1) == 0)
    def _():
        tail_ref[...] = jnp.zeros_like(tail_ref)

    xb = x_ref[...].astype(BF16)

    def conv_cols(part, lo):
        sl = slice(lo, lo + cols)
        u = _dot(xb, wup_ref[:, sl])
        ubuf_ref[part, 0:pad, :] = tail_ref[:, sl]
        ubuf_ref[part, pad:pad + tm, :] = u
        tail_ref[:, sl] = u[tm - pad:tm, :]
        u1 = ubuf_ref[part, pad - 1:pad - 1 + tm, :]
        u2 = ubuf_ref[part, pad - 2:pad - 2 + tm, :]
        return cw_ref[2:3, sl] * u + cw_ref[1:2, sl] * u1 + cw_ref[0:1, sl] * u2 + cw_ref[3:4, sl]

    for lo in range(0, f, cols):
        a = conv_cols(0, lo)
        gate = conv_cols(1, f + lo)
        act_ref[:, lo:lo + cols] = (a * _gelu_tanh(gate)).astype(BF16)
    h = _dot(act_ref[...], wdn_ref[...])
    o_ref[...] = _layer_norm(alpha * x_ref[...] + h, g_ref[...], b_ref[...])


def _ffn_layer(x, batch, w_up, conv_w, conv_b, w_down, g, b, alpha, tm):
    t, d = x.shape
    seq = t // batch
    f = w_down.shape[0]
    x3 = x.reshape(batch, seq, d)
    cw = jnp.concatenate([conv_w, conv_b[None, :], jnp.zeros((4, 2 * f), F32)], axis=0)
    tok = pl.BlockSpec((None, tm, d), lambda bi, i: (bi, i, 0))
    vec = pl.BlockSpec((1, d), lambda bi, i: (0, 0))
    out = pl.pallas_call(
        functools.partial(_ffn_kernel, alpha=alpha, cols=FFN_COLS),
        grid=(batch, seq // tm),
        in_specs=[tok,
                  pl.BlockSpec((d, 2 * f), lambda bi, i: (0, 0)),
                  pl.BlockSpec((8, 2 * f), lambda bi, i: (0, 0)),
                  pl.BlockSpec((f, d), lambda bi, i: (0, 0)),
                  vec, vec],
        out_specs=tok,
        out_shape=jax.ShapeDtypeStruct((batch, seq, d), F32),
        scratch_shapes=[pltpu.VMEM((2, tm + 8, FFN_COLS), F32),
                        pltpu.VMEM((8, 2 * f), F32),
                        pltpu.VMEM((tm, f), BF16)],
        compiler_params=_params("parallel", "arbitrary"),
        name="ffn",
    )(x3, w_up.astype(BF16), cw, w_down.astype(BF16), g, b)
    return out.reshape(t, d)


def kernel(x, mem, s5_w_in, s5_a_re, s5_a_im, s5_log_step, s5_b_re, s5_b_im, s5_c_re, s5_c_im, s5_d, s5_w_out,
           sb_w_qkv, sb_w_o, xa_w_q, xa_w_kv, xa_w_o, ffn_w_up, ffn_conv_w, ffn_conv_b, ffn_w_down, ln_g, ln_b):
    batch, seq, d = x.shape
    depth = ln_g.shape[0]
    alpha = (2.0 * depth) ** 0.25
    mem_len = mem.shape[1]
    xf = x.reshape(batch * seq, d)
    mem_f = mem.reshape(batch * mem_len, d)
    for i in range(depth):
        j = i // 2
        g = ln_g[i].reshape(3, 1, d)
        b = ln_b[i].reshape(3, 1, d)
        if i % 2 == 0:
            front = _s5_front(xf, batch, s5_w_in[j], s5_a_re[j], s5_a_im[j], s5_log_step[j], s5_b_re[j], s5_b_im[j],
                              s5_c_re[j], s5_c_im[j], s5_d[j], s5_w_out[j], g[0], b[0])
        else:
            front = _sb_front(xf, batch, sb_w_qkv[j], sb_w_o[j], g[0], b[0])
        kv = _matmul(mem_f, xa_w_kv[i].astype(BF16), BF16, PROJ_TM)
        xf = _mixer_xa(front, xf, batch, kv, xa_w_q[i], xa_w_o[i], g[1], b[1], alpha, XA_TM)
        xf = _ffn_layer(xf, batch, ffn_w_up[i], ffn_conv_w[i], ffn_conv_b[i], ffn_w_down[i], g[2], b[2], alpha, FFN_TM)
    return xf.reshape(batch, seq, d)
```

```python
import functools

import jax
import jax.numpy as jnp
from jax import lax
from jax.experimental import pallas as pl
from jax.experimental.pallas import tpu as pltpu

F32 = jnp.float32
BF16 = jnp.bfloat16

LN_EPS = 1e-5
LANES = 128
SB_HEADS = 16
XA_HEADS = 4
PROJ_TM = 512
S5_CHUNK = 16
S5_TM = 512
S5_IN_PARTS = 2
S5_GROUPS_PER_STEP = 4
XA_TM = 512
XA_PARTS = 2
FFN_COLS = 256
FFN_TM = 512
SB_SKIP_LOGIT = 105.0
SB_MASKED = -1e30
SB_WINDOW = 4
SB_SUB = 32
VMEM_LIMIT = 56 * 1024 * 1024


def _params(*sem):
    return pltpu.CompilerParams(dimension_semantics=sem, vmem_limit_bytes=VMEM_LIMIT)


def _layer_norm(y, g, b):
    mu = jnp.mean(y, axis=-1, keepdims=True)
    yc = y - mu
    var = jnp.mean(yc * yc, axis=-1, keepdims=True)
    return yc * lax.rsqrt(var + LN_EPS) * g + b


def _gelu_tanh(x):
    return 0.5 * x * (1.0 + jnp.tanh(0.7978845608028654 * (x + 0.044715 * (x * x * x))))


def _sigmoid(x):
    return 1.0 / (1.0 + jnp.exp(-x))


def _dot(a, b):
    return jnp.dot(a, b, preferred_element_type=F32)


def _dot_nt(a, b):
    return lax.dot_general(a, b, (((1,), (1,)), ((), ())), preferred_element_type=F32)


def _matmul_kernel(x_ref, w_ref, o_ref):
    o_ref[...] = _dot(x_ref[...].astype(BF16), w_ref[...]).astype(o_ref.dtype)


def _matmul(x, w, out_dtype, tm):
    t, k = x.shape
    n = w.shape[1]
    return pl.pallas_call(
        _matmul_kernel,
        grid=(t // tm,),
        in_specs=[pl.BlockSpec((tm, k), lambda i: (i, 0)),
                  pl.BlockSpec((k, n), lambda i: (0, 0))],
        out_specs=pl.BlockSpec((tm, n), lambda i: (i, 0)),
        out_shape=jax.ShapeDtypeStruct((t, n), out_dtype),
        compiler_params=_params("parallel"),
        name="matmul",
    )(x, w)


def _cmul(ar, ai, br, bi):
    return ar * br - ai * bi, ar * bi + ai * br


def _s5_tables(a_re, a_im, log_step, b_re, b_im, c_re, c_im, n_levels):
    hi = lax.Precision.HIGHEST
    ch = S5_CHUNK
    g_, p_ = a_re.shape
    n = b_re.shape[-1]
    step = jnp.exp(log_step)[:, None]
    mag = jnp.exp(a_re * step)
    ang = a_im * step
    lam_r = mag * jnp.cos(ang)
    lam_i = mag * jnp.sin(ang)
    den = a_re * a_re + a_im * a_im
    nr = lam_r - 1.0
    ni = lam_i
    f_r = (nr * a_re + ni * a_im) / den
    f_i = (ni * a_re - nr * a_im) / den
    bb_r = f_r[..., None] * b_re - f_i[..., None] * b_im
    bb_i = f_r[..., None] * b_im + f_i[..., None] * b_re
    pr, pi = [jnp.ones_like(lam_r)], [jnp.zeros_like(lam_i)]
    for _ in range(ch):
        r, i = _cmul(pr[-1], pi[-1], lam_r, lam_i)
        pr.append(r)
        pi.append(i)
    pw_r = jnp.stack(pr)
    pw_i = jnp.stack(pi)
    cp_r, cp_i = _cmul(c_re[None], c_im[None], pw_r[:ch, :, None, :], pw_i[:ch, :, None, :])
    kern = (jnp.einsum("dgcp,gpe->dgce", cp_r, bb_r, precision=hi)
            - jnp.einsum("dgcp,gpe->dgce", cp_i, bb_i, precision=hi))
    s_idx = jnp.arange(ch)[:, None]
    t_idx = jnp.arange(ch)[None, :]
    lag = jnp.clip(t_idx - s_idx, 0, ch - 1)
    toep = jnp.where((t_idx >= s_idx)[:, :, None, None, None], kern[lag], 0.0)
    toep = toep.transpose(2, 0, 4, 1, 3).reshape(g_, ch * n, ch * n)
    rev_r = pw_r[ch - 1 - jnp.arange(ch)]
    rev_i = pw_i[ch - 1 - jnp.arange(ch)]
    wi_r, wi_i = _cmul(rev_r[:, :, :, None], rev_i[:, :, :, None], bb_r[None], bb_i[None])
    w_in = jnp.stack([wi_r, wi_i], axis=0).transpose(2, 1, 4, 0, 3).reshape(g_, ch * n, 2 * p_)
    wo_r, wo_i = _cmul(c_re[None], c_im[None], pw_r[1:, :, None, :], pw_i[1:, :, None, :])
    w_out = jnp.stack([wo_r, -wo_i], axis=0).transpose(2, 0, 4, 1, 3).reshape(g_, 2 * p_, ch * n)
    lv = []
    lr, li = pw_r[ch], pw_i[ch]
    for _ in range(n_levels):
        lv.append(jnp.concatenate([lr, lr], axis=-1))
        lv.append(jnp.concatenate([-li, li], axis=-1))
        lr, li = _cmul(lr, li, lr, li)
    lam_lv = jnp.stack(lv, axis=1)
    return toep.astype(BF16), w_in.astype(BF16), w_out.astype(BF16), lam_lv.astype(F32)


def _s5_in_kernel(x_ref, w_ref, u_ref, ulay_ref, us_ref, *, ch, n, parts):
    tm, d = u_ref.shape
    nr = tm // parts
    nk = nr // ch
    per_tile = LANES // n
    slot = lax.broadcasted_iota(jnp.int32, (nk, LANES), 1) // n
    for p in range(parts):
        rows = slice(p * nr, (p + 1) * nr)
        u = _dot(x_ref[rows, :].astype(BF16), w_ref[...])
        u_ref[rows, :] = u
        for tile in range(d // LANES):
            us_ref[tile, rows, :] = u[:, tile * LANES:(tile + 1) * LANES]
    for p in range(parts):
        for tile in range(d // LANES):
            phases = [us_ref[tile, pl.ds(p * nr + s, nk, stride=ch), :] for s in range(ch)]
            for gl in range(per_tile):
                for half in range(ch // per_tile):
                    acc = None
                    for j in range(per_tile):
                        src = phases[half * per_tile + j]
                        shift = ((j - gl) % per_tile) * n
                        r = pltpu.roll(src, shift, axis=1) if shift else src
                        acc = r if acc is None else jnp.where(slot == j, r, acc)
                    ulay_ref[tile * per_tile + gl, p * nk:(p + 1) * nk, half * LANES:(half + 1) * LANES] = (
                        acc.astype(BF16))


def _s5_in(x, w_in, g_, tm):
    t, d = x.shape
    n = d // g_
    ch = S5_CHUNK
    nk = tm // ch
    return pl.pallas_call(
        functools.partial(_s5_in_kernel, ch=ch, n=n, parts=S5_IN_PARTS),
        grid=(t // tm,),
        in_specs=[pl.BlockSpec((tm, d), lambda i: (i, 0)),
                  pl.BlockSpec((d, d), lambda i: (0, 0))],
        out_specs=[pl.BlockSpec((tm, d), lambda i: (i, 0)),
                   pl.BlockSpec((g_, nk, ch * n), lambda i: (0, i, 0))],
        out_shape=[jax.ShapeDtypeStruct((t, d), F32),
                   jax.ShapeDtypeStruct((g_, t // ch, ch * n), BF16)],
        scratch_shapes=[pltpu.VMEM((d // LANES, tm, LANES), F32)],
        compiler_params=_params("parallel"),
        name="s5_in",
    )(x, w_in)


def _s5_ssm_kernel(u_ref, toep_ref, win_ref, wout_ref, lam_ref, y_ref):
    n_chunks = u_ref.shape[1]
    two_p = win_ref.shape[2]
    row = lax.broadcasted_iota(jnp.int32, (n_chunks, two_p), 0)
    for gi in range(u_ref.shape[0]):
        u = u_ref[gi]
        intra = _dot(u, toep_ref[gi])
        h = _dot(u, win_ref[gi])
        shift, level = 1, 0
        while shift < n_chunks:
            a = lam_ref[gi, 2 * level:2 * level + 1, :]
            b = lam_ref[gi, 2 * level + 1:2 * level + 2, :]
            hs = jnp.where(row >= shift, pltpu.roll(h, shift, axis=0), 0.0)
            h = h + a * hs + b * pltpu.roll(hs, two_p // 2, axis=1)
            shift *= 2
            level += 1
        h_prev = jnp.where(row >= 1, pltpu.roll(h, 1, axis=0), 0.0)
        y_ref[gi] = intra + _dot(h_prev.astype(BF16), wout_ref[gi])


def _s5_ssm(u_lay, tables, batch):
    toep, w_in, w_out, lam_lv = tables
    g_, rows, width = u_lay.shape
    nc = rows // batch
    two_p = w_in.shape[-1]
    gb = S5_GROUPS_PER_STEP
    return pl.pallas_call(
        _s5_ssm_kernel,
        grid=(g_ // gb, batch),
        in_specs=[pl.BlockSpec((gb, nc, width), lambda g, b: (g, b, 0)),
                  pl.BlockSpec((gb, width, width), lambda g, b: (g, 0, 0)),
                  pl.BlockSpec((gb, width, two_p), lambda g, b: (g, 0, 0)),
                  pl.BlockSpec((gb, two_p, width), lambda g, b: (g, 0, 0)),
                  pl.BlockSpec((gb, lam_lv.shape[1], two_p), lambda g, b: (g, 0, 0))],
        out_specs=pl.BlockSpec((gb, nc, width), lambda g, b: (g, b, 0)),
        out_shape=jax.ShapeDtypeStruct((g_, rows, width), F32),
        compiler_params=_params("parallel", "parallel"),
        name="s5_ssm",
    )(u_lay, toep, w_in, w_out, lam_lv)


def _s5_mix(ylay_ref, u_ref, x_ref, d_ref, w_ref, g_ref, b_ref, ys_ref, r0, nr, *, alpha, ch, n):
    d_model = x_ref.shape[1]
    nk, k0 = nr // ch, r0 // ch
    per_tile = LANES // n
    slot = lax.broadcasted_iota(jnp.int32, (nk, LANES), 1) // n
    for tile in range(d_model // LANES):
        for t in range(ch):
            half, tj = divmod(t, per_tile)
            acc = None
            for gl in range(per_tile):
                src = ylay_ref[tile * per_tile + gl, k0:k0 + nk, half * LANES:(half + 1) * LANES]
                shift = ((gl - tj) % per_tile) * n
                r = pltpu.roll(src, shift, axis=1) if shift else src
                acc = r if acc is None else jnp.where(slot == gl, r, acc)
            ys_ref[tile, pl.ds(r0 + t, nk, stride=ch), :] = acc
    y = jnp.concatenate([ys_ref[tile, r0:r0 + nr, :] for tile in range(d_model // LANES)], axis=1)
    v = _gelu_tanh(y + d_ref[...] * u_ref[r0:r0 + nr, :])
    hw = _dot(v.astype(BF16), w_ref[...])
    h = hw[:, :d_model] * _sigmoid(hw[:, d_model:])
    return _layer_norm(alpha * x_ref[r0:r0 + nr, :] + h, g_ref[...], b_ref[...])


def _s5_front(x, batch, w_in, a_re, a_im, log_step, b_re, b_im, c_re, c_im, d_skip, w_out, g, b):
    t, d = x.shape
    seq = t // batch
    g_ = a_re.shape[0]
    n_chunks = seq // S5_CHUNK
    n_levels = max(1, (n_chunks - 1).bit_length())
    tables = _s5_tables(a_re, a_im, log_step, b_re, b_im, c_re, c_im, n_levels)
    u, u_lay = _s5_in(x, w_in.astype(BF16), g_, S5_TM)
    y_lay = _s5_ssm(u_lay, tables, batch)
    return ("s5", y_lay, u, d_skip.reshape(1, d), w_out.astype(BF16), g, b)


def _sb_attn_kernel(q_ref, k_ref, v_ref, tri_ref, o_ref, acc_ref, carry_ref, *, blk, window, sub):
    seq = q_ref.shape[0]
    lane = lax.broadcasted_iota(jnp.int32, (blk, LANES), 1)
    row = lax.broadcasted_iota(jnp.int32, (blk, LANES), 0)
    diag_visible = (lane & (blk - 1)) < row
    lane2 = lax.broadcasted_iota(jnp.int32, (2 * blk, LANES), 1)
    row2 = lax.broadcasted_iota(jnp.int32, (2 * blk, LANES), 0)
    own_head = (lane2 < blk) == (row2 < blk)

    def load_kv(j):
        start = pl.multiple_of(jnp.maximum(j, 0) * blk, blk)
        return k_ref[pl.ds(start, blk), :], v_ref[pl.ds(start, blk), :]

    def per_head_rows(x):
        x2 = jnp.concatenate([x, x], axis=0)
        return jnp.where(own_head, x2, jnp.zeros_like(x2))

    def softplus_split(z2):
        sp = jnp.maximum(z2, 0.0) + jnp.log(1.0 + jnp.exp(-jnp.abs(z2)))
        hi = sp.astype(BF16)
        return jnp.concatenate([hi, (sp - hi.astype(F32)).astype(BF16)], axis=1)

    def near_blocks(it):
        base = it * sub
        z, vals = {}, {}
        for off in range(1 - window, sub):
            s_lo, s_hi = max(off, 0), min(off + window - 1, sub - 1)
            j = base + off
            kb, vb = load_kv(j)
            vals[off] = per_head_rows(vb)
            rows = pl.ds(pl.multiple_of((base + s_lo) * blk, blk), (s_hi - s_lo + 1) * blk)
            zz = _dot_nt(q_ref[rows, :], per_head_rows(kb))
            if off < 0:
                zz = zz + jnp.where(j >= 0, 0.0, SB_MASKED)
            for s_ in range(s_lo, s_hi + 1):
                piece = zz[(s_ - s_lo) * blk:(s_ - s_lo + 1) * blk]
                z[s_, s_ - off] = jnp.where(diag_visible, piece, SB_MASKED) if s_ == off else piece
        chains = [(s_, d) for s_ in range(sub) for d in range(window)]
        sums = _dot(jnp.concatenate([softplus_split(z[c]) for c in chains], axis=0), tri_ref[...])
        lows = []
        for s_ in range(sub):
            offs = jnp.zeros((blk, LANES), F32)
            ws = []
            for d in range(window):
                sm = sums[(s_ * window + d) * blk:(s_ * window + d + 1) * blk]
                ws.append(jnp.exp(z[s_, d] - sm[:, :LANES] - offs).astype(BF16))
                offs = offs + sm[:, LANES:]
            vcat = jnp.concatenate([vals[s_ - d] for d in range(window)], axis=0)
            acc_ref[s_] = _dot(jnp.concatenate(ws, axis=1), vcat)
            carry_ref[s_] = offs
            lows.append(offs)
        return [jnp.min(o) for o in lows]

    def far_blocks(ib, slot, low):
        def cond(state):
            j, lo_ = state
            return jnp.logical_and(j >= 0, lo_ < SB_SKIP_LOGIT)

        def body(state):
            j, _ = state
            kb, vb = load_kv(j)
            q = q_ref[pl.ds(pl.multiple_of(ib * blk, blk), blk), :]
            z2 = _dot_nt(q, per_head_rows(kb))
            sm = _dot(softplus_split(z2), tri_ref[...])
            carry = carry_ref[slot]
            w = jnp.exp(z2 - sm[:, :LANES] - carry).astype(BF16)
            acc_ref[slot] += _dot(w, per_head_rows(vb))
            carry = carry + sm[:, LANES:]
            carry_ref[slot] = carry
            return j - 1, jnp.min(carry)

        lax.while_loop(cond, body, (ib - window, low))

    def step(it, carry):
        lows = near_blocks(it)
        for s_ in range(sub):
            far_blocks(it * sub + s_, s_, lows[s_])
        for s_ in range(sub):
            r0 = pl.multiple_of((it * sub + s_) * blk, blk)
            o_ref[pl.ds(r0, blk), :] = acc_ref[s_].astype(o_ref.dtype)
        return carry

    lax.fori_loop(0, seq // (blk * sub), step, 0)


def _sb_attention(qkv, batch, d_model):
    t = qkv.shape[0]
    seq = t // batch
    blk = d_model // SB_HEADS
    pairs = d_model // LANES
    qkv3 = qkv.reshape(batch, seq, 3 * d_model)
    r = jnp.arange(2 * blk)
    same_head = (r[:, None] // blk) == (r[None, :] // blk)
    suffix = same_head & (r[:, None] >= r[None, :])
    tri = jnp.concatenate([suffix, same_head], axis=1).astype(BF16)
    tri = jnp.concatenate([tri, tri], axis=0)
    out = pl.pallas_call(
        functools.partial(_sb_attn_kernel, blk=blk, window=SB_WINDOW, sub=SB_SUB),
        grid=(batch, pairs),
        in_specs=[pl.BlockSpec((None, seq, LANES), lambda b, p: (b, 0, p)),
                  pl.BlockSpec((None, seq, LANES), lambda b, p: (b, 0, pairs + p)),
                  pl.BlockSpec((None, seq, LANES), lambda b, p: (b, 0, 2 * pairs + p)),
                  pl.BlockSpec((4 * blk, 4 * blk), lambda b, p: (0, 0))],
        out_specs=pl.BlockSpec((None, seq, LANES), lambda b, p: (b, 0, p)),
        out_shape=jax.ShapeDtypeStruct((batch, seq, d_model), BF16),
        scratch_shapes=[pltpu.VMEM((SB_SUB, blk, LANES), F32), pltpu.VMEM((SB_SUB, blk, LANES), F32)],
        compiler_params=_params("parallel", "parallel"),
        name="sb_attn",
    )(qkv3, qkv3, qkv3, tri)
    return out.reshape(t, d_model)


def _sb_front(x, batch, w_qkv, w_o, g, b):
    d = x.shape[1]
    scale = (d // SB_HEADS) ** -0.5
    col_scale = jnp.concatenate([jnp.full((d,), scale, F32), jnp.ones((2 * d,), F32)])
    qkv = _matmul(x, (w_qkv * col_scale).astype(BF16), BF16, PROJ_TM)
    return ("sb", _sb_attention(qkv, batch, d), w_o.astype(BF16), g, b)


def _xa_core(x, wq_ref, kv_ref, wo_ref, g_ref, b_ref, o_ref, r0, *, alpha, heads):
    nr, d_model = x.shape
    dh = d_model // heads
    q = _dot(x.astype(BF16), wq_ref[...]).astype(BF16)
    outs = []
    for h in range(heads):
        kh = kv_ref[:, h * dh:(h + 1) * dh]
        vh = kv_ref[:, d_model + h * dh:d_model + (h + 1) * dh]
        s = _dot_nt(q[:, h * dh:(h + 1) * dh], kh)
        p = jnp.exp(s - jnp.max(s, axis=-1, keepdims=True))
        p = p / jnp.sum(p, axis=-1, keepdims=True)
        outs.append(_dot(p.astype(BF16), vh).astype(BF16))
    h_out = _dot(jnp.concatenate(outs, axis=1), wo_ref[...])
    o_ref[r0:r0 + nr, :] = _layer_norm(alpha * x + h_out, g_ref[...], b_ref[...])


def _sb_xa_kernel(a_ref, x_ref, wpre_ref, gpre_ref, bpre_ref, wq_ref, kv_ref, wo_ref, g_ref, b_ref, o_ref,
                  *, alpha, heads, parts):
    nr = x_ref.shape[0] // parts
    xs = []
    for p in range(parts):
        rows = slice(p * nr, (p + 1) * nr)
        xs.append(_layer_norm(alpha * x_ref[rows, :] + _dot(a_ref[rows, :], wpre_ref[...]),
                              gpre_ref[...], bpre_ref[...]))
    for p in range(parts):
        _xa_core(xs[p], wq_ref, kv_ref, wo_ref, g_ref, b_ref, o_ref, p * nr, alpha=alpha, heads=heads)


def _s5_xa_kernel(ylay_ref, u_ref, x_ref, d_ref, wpre_ref, gpre_ref, bpre_ref, wq_ref, kv_ref, wo_ref, g_ref, b_ref,
                  o_ref, ys_ref, *, alpha, heads, ch, n, parts):
    nr = x_ref.shape[0] // parts
    xs = [_s5_mix(ylay_ref, u_ref, x_ref, d_ref, wpre_ref, gpre_ref, bpre_ref, ys_ref, p * nr, nr,
                  alpha=alpha, ch=ch, n=n) for p in range(parts)]
    for p in range(parts):
        _xa_core(xs[p], wq_ref, kv_ref, wo_ref, g_ref, b_ref, o_ref, p * nr, alpha=alpha, heads=heads)


def _mixer_xa(front, x, batch, kv, w_q, w_o, g, b, alpha, tm):
    t, d = x.shape
    seq = t // batch
    tiles = seq // tm
    mem_len = kv.shape[0] // batch
    scale = (d // XA_HEADS) ** -0.5
    tok = pl.BlockSpec((tm, d), lambda bi, i: (bi * tiles + i, 0))
    vec = pl.BlockSpec((1, d), lambda bi, i: (0, 0))
    mat = pl.BlockSpec((d, d), lambda bi, i: (0, 0))
    xa_specs = [mat, pl.BlockSpec((None, mem_len, 2 * d), lambda bi, i: (bi, 0, 0)), mat, vec, vec]
    xa_args = ((w_q * scale).astype(BF16), kv.reshape(batch, mem_len, 2 * d), w_o.astype(BF16), g, b)
    if front[0] == "sb":
        _, a, w_pre, g0, b0 = front
        body = functools.partial(_sb_xa_kernel, alpha=alpha, heads=XA_HEADS, parts=XA_PARTS)
        specs = [tok, tok, mat, vec, vec]
        args = (a, x, w_pre, g0, b0)
        scratch = []
    else:
        _, y_lay, u, d_skip, w_pre, g0, b0 = front
        g_, _, width = y_lay.shape
        ch = S5_CHUNK
        body = functools.partial(_s5_xa_kernel, alpha=alpha, heads=XA_HEADS, ch=ch, n=width // ch, parts=XA_PARTS)
        specs = [pl.BlockSpec((g_, tm // ch, width), lambda bi, i: (0, bi * tiles + i, 0)),
                 tok, tok, vec, pl.BlockSpec((d, 2 * d), lambda bi, i: (0, 0)), vec, vec]
        args = (y_lay, u, x, d_skip, w_pre, g0, b0)
        scratch = [pltpu.VMEM((d // LANES, tm, LANES), F32)]
    return pl.pallas_call(
        body,
        grid=(batch, tiles),
        in_specs=specs + xa_specs,
        out_specs=tok,
        out_shape=jax.ShapeDtypeStruct((t, d), F32),
        scratch_shapes=scratch,
        compiler_params=_params("parallel", "parallel"),
        name=front[0] + "_xa",
    )(*args, *xa_args)


def _ffn_kernel(x_ref, wup_ref, cw_ref, wdn_ref, g_ref, b_ref, o_ref, ubuf_ref, tail_ref, act_ref, *, alpha, cols):
    tm = x_ref.shape[0]
    f = wdn_ref.shape[0]
    pad = tail_ref.shape[0]

    @pl.when(pl.program_id(1) == 0)
    def _():
        tail_ref[...] = jnp.zeros_like(tail_ref)

    xb = x_ref[...].astype(BF16)

    def conv_cols(part, lo):
        sl = slice(lo, lo + cols)
        u = _dot(xb, wup_ref[:, sl])
        ubuf_ref[part, 0:pad, :] = tail_ref[:, sl]
        ubuf_ref[part, pad:pad + tm, :] = u
        tail_ref[:, sl] = u[tm - pad:tm, :]
        u1 = ubuf_ref[part, pad - 1:pad - 1 + tm, :]
        u2 = ubuf_ref[part, pad - 2:pad - 2 + tm, :]
        return cw_ref[2:3, sl] * u + cw_ref[1:2, sl] * u1 + cw_ref[0:1, sl] * u2 + cw_ref[3:4, sl]

    for lo in range(0, f, cols):
        a = conv_cols(0, lo)
        gate = conv_cols(1, f + lo)
        act_ref[:, lo:lo + cols] = (a * _gelu_tanh(gate)).astype(BF16)
    h = _dot(act_ref[...], wdn_ref[...])
    o_ref[...] = _layer_norm(alpha * x_ref[...] + h, g_ref[...], b_ref[...])


def _ffn_layer(x, batch, w_up, conv_w, conv_b, w_down, g, b, alpha, tm):
    t, d = x.shape
    seq = t // batch
    f = w_down.shape[0]
    x3 = x.reshape(batch, seq, d)
    cw = jnp.concatenate([conv_w, conv_b[None, :], jnp.zeros((4, 2 * f), F32)], axis=0)
    tok = pl.BlockSpec((None, tm, d), lambda bi, i: (bi, i, 0))
    vec = pl.BlockSpec((1, d), lambda bi, i: (0, 0))
    out = pl.pallas_call(
        functools.partial(_ffn_kernel, alpha=alpha, cols=FFN_COLS),
        grid=(batch, seq // tm),
        in_specs=[tok,
                  pl.BlockSpec((d, 2 * f), lambda bi, i: (0, 0)),
                  pl.BlockSpec((8, 2 * f), lambda bi, i: (0, 0)),
                  pl.BlockSpec((f, d), lambda bi, i: (0, 0)),
                  vec, vec],
        out_specs=tok,
        out_shape=jax.ShapeDtypeStruct((batch, seq, d), F32),
        scratch_shapes=[pltpu.VMEM((2, tm + 8, FFN_COLS), F32),
                        pltpu.VMEM((8, 2 * f), F32),
                        pltpu.VMEM((tm, f), BF16)],
        compiler_params=_params("parallel", "arbitrary"),
        name="ffn",
    )(x3, w_up.astype(BF16), cw, w_down.astype(BF16), g, b)
    return out.reshape(t, d)


def kernel(x, mem, s5_w_in, s5_a_re, s5_a_im, s5_log_step, s5_b_re, s5_b_im, s5_c_re, s5_c_im, s5_d, s5_w_out,
           sb_w_qkv, sb_w_o, xa_w_q, xa_w_kv, xa_w_o, ffn_w_up, ffn_conv_w, ffn_conv_b, ffn_w_down, ln_g, ln_b):
    batch, seq, d = x.shape
    depth = ln_g.shape[0]
    alpha = (2.0 * depth) ** 0.25
    mem_len = mem.shape[1]
    xf = x.reshape(batch * seq, d)
    mem_f = mem.reshape(batch * mem_len, d)
    for i in range(depth):
        j = i // 2
        g = ln_g[i].reshape(3, 1, d)
        b = ln_b[i].reshape(3, 1, d)
        if i % 2 == 0:
            front = _s5_front(xf, batch, s5_w_in[j], s5_a_re[j], s5_a_im[j], s5_log_step[j], s5_b_re[j], s5_b_im[j],
                              s5_c_re[j], s5_c_im[j], s5_d[j], s5_w_out[j], g[0], b[0])
        else:
            front = _sb_front(xf, batch, sb_w_qkv[j], sb_w_o[j], g[0], b[0])
        kv = _matmul(mem_f, xa_w_kv[i].astype(BF16), BF16, PROJ_TM)
        xf = _mixer_xa(front, xf, batch, kv, xa_w_q[i], xa_w_o[i], g[1], b[1], alpha, XA_TM)
        xf = _ffn_layer(xf, batch, ffn_w_up[i], ffn_conv_w[i], ffn_conv_b[i], ffn_w_down[i], g[2], b[2], alpha, FFN_TM)
    return xf.reshape(batch, seq, d)
```
